```python
import jax, jax.numpy as jnp
from jax import lax
import numpy as np

D_MODEL = 1024
BATCH = 8
SEQ = 4096
DEPTH = 4

N_META = 16
WINDOW = 128
BLOCK = 128
PREFIX = BLOCK
PAD_FRONT = PREFIX - N_META
ATT_HEADS = 8
ATT_KV_HEADS = 2
ATT_HEAD_DIM = 64
ATT_GROUP = ATT_HEADS // ATT_KV_HEADS
ATT_Q = ATT_HEADS * ATT_HEAD_DIM
ATT_KV = ATT_KV_HEADS * ATT_HEAD_DIM
DN_HEADS = 4
DN_HEAD_DIM = 128
DN_W = DN_HEADS * DN_HEAD_DIM
DN_CHUNK = 64
CONV_WIDTH = 4
D_FF = -(-8 * D_MODEL // (3 * 256)) * 256
RMS_EPS = 1e-6
IN_SIZES = (ATT_Q, ATT_KV, ATT_KV, DN_W, DN_W, DN_W, DN_W, DN_HEADS, DN_HEADS, D_MODEL, D_MODEL)
IN_WIDTH = ATT_Q + 2 * ATT_KV + 4 * DN_W + 2 * DN_HEADS + 2 * D_MODEL

kernel_name = "hybrid_swa_sink_gated_deltanet_parallel"


def rmsnorm(x, w):
    xf = x.astype(jnp.float32)
    y = xf * lax.rsqrt(jnp.mean(xf * xf, axis=-1, keepdims=True) + RMS_EPS)
    return (y * w.astype(jnp.float32)).astype(x.dtype)


def l2norm(x):
    xf = x.astype(jnp.float32)
    return xf * lax.rsqrt(jnp.sum(xf * xf, axis=-1, keepdims=True) + RMS_EPS)


def sliding_window_attention(q, k, v, sinks):
    B, L = q.shape[:2]
    nb = L // BLOCK
    f32 = jnp.float32
    qb = q.astype(f32).reshape(B, nb, BLOCK, ATT_KV_HEADS, ATT_GROUP, ATT_HEAD_DIM) * (ATT_HEAD_DIM ** -0.5)

    def keys_for_blocks(t):
        t = t.astype(f32)
        tb = t.reshape(B, nb, BLOCK, ATT_KV_HEADS, ATT_HEAD_DIM)
        prev = jnp.concatenate([jnp.zeros_like(tb[:, :1]), tb[:, :-1]], axis=1)
        meta = jnp.broadcast_to(t[:, None, PAD_FRONT:PREFIX], (B, nb, N_META, ATT_KV_HEADS, ATT_HEAD_DIM))
        return jnp.concatenate([meta, prev, tb], axis=2)

    kk = keys_for_blocks(k)
    vv = keys_for_blocks(v)
    s = jnp.einsum('bnqhgd,bnkhd->bhgnqk', qb, kk)

    blk = jnp.arange(nb)[:, None]
    q_pos = blk * BLOCK + jnp.arange(BLOCK)[None, :]
    band_pos = (blk - 1) * BLOCK + jnp.arange(2 * BLOCK)[None, :]
    meta_pos = PAD_FRONT + jnp.arange(N_META)
    diff = q_pos[:, :, None] - band_pos[:, None, :]
    band_ok = (diff >= 0) & (diff < WINDOW) & (band_pos[:, None, :] >= PREFIX)
    meta_ok = jnp.broadcast_to(meta_pos[None, None, :] <= q_pos[:, :, None], (nb, BLOCK, N_META))
    mask = jnp.concatenate([meta_ok, band_ok], axis=-1)
    s = jnp.where(mask, s, -jnp.inf)

    sink = sinks.astype(f32).reshape(ATT_KV_HEADS, ATT_GROUP)[None, :, :, None, None, None]
    m = jnp.maximum(jnp.max(s, axis=-1, keepdims=True), sink)
    p = jnp.exp(s - m)
    probs = p / (jnp.sum(p, axis=-1, keepdims=True) + jnp.exp(sink - m))
    o = jnp.einsum('bhgnqk,bnkhd->bnqhgd', probs, vv)
    return o.reshape(B, L, ATT_Q).astype(q.dtype)


def causal_depthwise_conv(x, w):
    C = x.shape[-1]
    return lax.conv_general_dilated(
        x, w[:, None, :].astype(x.dtype), window_strides=(1,), padding=[(CONV_WIDTH - 1, 0)],
        dimension_numbers=('NWC', 'WIO', 'NWC'), feature_group_count=C)


def gated_delta_rule(q, k, v, g, beta):
    f32 = jnp.float32
    B, L, H, dk = q.shape
    dv = v.shape[-1]
    nc = L // DN_CHUNK
    C = DN_CHUNK

    def chunk(t):
        t = t.astype(f32).reshape((B, nc, C, H) + t.shape[3:])
        return jnp.moveaxis(t, 3, 1)

    q = chunk(q) * (dk ** -0.5)
    k = chunk(k)
    v = chunk(v)
    g = chunk(g)
    beta = chunk(beta)
    gcum = jnp.cumsum(g, axis=-1)
    tril = jnp.tril(jnp.ones((C, C), dtype=bool))
    strict = jnp.tril(jnp.ones((C, C), dtype=bool), -1)
    decay = jnp.exp(jnp.where(tril, gcum[..., :, None] - gcum[..., None, :], -jnp.inf))
    k_beta = k * beta[..., None]
    a = jnp.where(strict, jnp.einsum('bhncd,bhnsd->bhncs', k_beta, k) * decay, 0.0)
    lhs = a + jnp.eye(C, dtype=f32)
    w = lax.linalg.triangular_solve(lhs, k_beta * jnp.exp(gcum)[..., None],
                                    left_side=True, lower=True, unit_diagonal=True)
    u = lax.linalg.triangular_solve(lhs, v * beta[..., None],
                                    left_side=True, lower=True, unit_diagonal=True)
    attn = jnp.einsum('bhncd,bhnsd->bhncs', q, k) * decay
    q_dec = q * jnp.exp(gcum)[..., None]
    g_last = gcum[..., -1]
    k_dec = k * jnp.exp(g_last[..., None] - gcum)[..., None]

    def step(S, xs):
        q_i, k_i, u_i, w_i, attn_i, gl_i = xs
        v_new = u_i - jnp.einsum('bhcd,bhde->bhce', w_i, S)
        o_i = jnp.einsum('bhcd,bhde->bhce', q_i, S) + jnp.einsum('bhcs,bhse->bhce', attn_i, v_new)
        S = S * jnp.exp(gl_i)[..., None, None] + jnp.einsum('bhcd,bhce->bhde', k_i, v_new)
        return S, o_i

    xs = (jnp.moveaxis(q_dec, 2, 0), jnp.moveaxis(k_dec, 2, 0), jnp.moveaxis(u, 2, 0),
          jnp.moveaxis(w, 2, 0), jnp.moveaxis(attn, 2, 0), jnp.moveaxis(g_last, 2, 0))
    S0 = jnp.zeros((B, H, dk, dv), f32)
    _, o = lax.scan(step, S0, xs)
    return jnp.transpose(o, (1, 0, 3, 2, 4)).reshape(B, L, H, dv)


def hybrid_layer(x, valid, w_in, b_in, conv_w, a_log, dt_bias, dn_norm_w, att_sinks,
                 w_att_out, w_dn_out, w_out, norm_mix_pre, norm_mix_post,
                 norm_ffn_pre, norm_ffn_post, w_ffn_in, w_ffn_out):
    B, L, _ = x.shape
    f32 = jnp.float32
    h = rmsnorm(x, norm_mix_pre)
    proj = (h @ w_in + b_in) * valid
    splits = np.cumsum(IN_SIZES)[:-1].tolist()
    aq, ak, av, dq, dk, dv, dz, da, db, ga, gb = jnp.split(proj, splits, axis=-1)

    y_att = sliding_window_attention(aq.reshape(B, L, ATT_HEADS, ATT_HEAD_DIM),
                                     ak.reshape(B, L, ATT_KV_HEADS, ATT_HEAD_DIM),
                                     av.reshape(B, L, ATT_KV_HEADS, ATT_HEAD_DIM), att_sinks)

    qkv = jax.nn.silu(causal_depthwise_conv(jnp.concatenate([dq, dk, dv], axis=-1), conv_w))
    cq, ck, cv = jnp.split(qkv, 3, axis=-1)
    q = l2norm(cq.reshape(B, L, DN_HEADS, DN_HEAD_DIM))
    k = l2norm(ck.reshape(B, L, DN_HEADS, DN_HEAD_DIM))
    v = cv.reshape(B, L, DN_HEADS, DN_HEAD_DIM).astype(f32)
    g = -jnp.exp(a_log.astype(f32)) * jax.nn.softplus(da.astype(f32) + dt_bias.astype(f32))
    beta = jax.nn.sigmoid(db.astype(f32))
    o = gated_delta_rule(q, k, v, g, beta)
    o = rmsnorm(o, dn_norm_w) * jax.nn.silu(dz.reshape(B, L, DN_HEADS, DN_HEAD_DIM).astype(f32))
    y_dn = o.reshape(B, L, DN_W).astype(x.dtype)

    merged = jax.nn.sigmoid(ga) * (y_att @ w_att_out) + jax.nn.sigmoid(gb) * (y_dn @ w_dn_out)
    x = x + valid * rmsnorm(merged @ w_out, norm_mix_post)

    hf = rmsnorm(x, norm_ffn_pre)
    gate, up = jnp.split(hf @ w_ffn_in, 2, axis=-1)
    x = x + valid * rmsnorm((jax.nn.silu(gate) * up) @ w_ffn_out, norm_ffn_post)
    return x


def setup_inputs(seed: int = 0) -> dict:
    key = jax.random.key(seed)
    ks = jax.random.split(key, 20)
    f32 = jnp.float32
    nrm = lambda k, shape, scale: jax.random.normal(k, shape, f32) * scale
    gain = lambda k, shape: 1.0 + 0.02 * jax.random.normal(k, shape, f32)
    dt = jnp.exp(jax.random.uniform(ks[5], (DEPTH, DN_HEADS), f32, jnp.log(1e-3), jnp.log(1e-1)))
    return {
        "x": nrm(ks[0], (BATCH, SEQ, D_MODEL), 1.0),
        "meta_tokens": nrm(ks[1], (N_META, D_MODEL), 1.0),
        "w_in": nrm(ks[2], (DEPTH, D_MODEL, IN_WIDTH), D_MODEL ** -0.5),
        "b_in": nrm(ks[3], (DEPTH, IN_WIDTH), 0.02),
        "conv_w": nrm(ks[4], (DEPTH, CONV_WIDTH, 3 * DN_W), CONV_WIDTH ** -0.5),
        "a_log": jnp.log(jax.random.uniform(ks[6], (DEPTH, DN_HEADS), f32, 1.0, 16.0)),
        "dt_bias": dt + jnp.log(-jnp.expm1(-dt)),
        "dn_norm_w": gain(ks[7], (DEPTH, DN_HEAD_DIM)),
        "att_sinks": nrm(ks[8], (DEPTH, ATT_HEADS), 1.0),
        "w_att_out": nrm(ks[9], (DEPTH, ATT_Q, D_MODEL), ATT_Q ** -0.5),
        "w_dn_out": nrm(ks[10], (DEPTH, DN_W, D_MODEL), DN_W ** -0.5),
        "w_out": nrm(ks[11], (DEPTH, D_MODEL, D_MODEL), D_MODEL ** -0.5),
        "norm_mix_pre": gain(ks[12], (DEPTH, D_MODEL)),
        "norm_mix_post": gain(ks[13], (DEPTH, D_MODEL)),
        "norm_ffn_pre": gain(ks[14], (DEPTH, D_MODEL)),
        "norm_ffn_post": gain(ks[15], (DEPTH, D_MODEL)),
        "w_ffn_in": nrm(ks[16], (DEPTH, D_MODEL, 2 * D_FF), D_MODEL ** -0.5),
        "w_ffn_out": nrm(ks[17], (DEPTH, D_FF, D_MODEL), D_FF ** -0.5),
    }


def reference(x, meta_tokens, w_in, b_in, conv_w, a_log, dt_bias, dn_norm_w, att_sinks,
              w_att_out, w_dn_out, w_out, norm_mix_pre, norm_mix_post, norm_ffn_pre,
              norm_ffn_post, w_ffn_in, w_ffn_out):
    B = x.shape[0]
    pads = jnp.zeros((B, PAD_FRONT, D_MODEL), x.dtype)
    meta = jnp.broadcast_to(meta_tokens.astype(x.dtype)[None], (B, N_META, D_MODEL))
    h = jnp.concatenate([pads, meta, x], axis=1)
    L = h.shape[1]
    valid = (jnp.arange(L) >= PAD_FRONT).astype(x.dtype)[None, :, None]
    for l in range(DEPTH):
        h = hybrid_layer(h, valid, w_in[l], b_in[l], conv_w[l], a_log[l], dt_bias[l], dn_norm_w[l],
                         att_sinks[l], w_att_out[l], w_dn_out[l], w_out[l], norm_mix_pre[l],
                         norm_mix_post[l], norm_ffn_pre[l], norm_ffn_post[l], w_ffn_in[l], w_ffn_out[l])
    return h[:, PREFIX:]
```

```python
import functools

import jax
import jax.numpy as jnp
from jax import lax
from jax.experimental import pallas as pl
from jax.experimental.pallas import tpu as pltpu

D_MODEL = 1024
N_META = 16
BLOCK = 128
PREFIX = BLOCK
PAD_FRONT = PREFIX - N_META
ATT_HEADS = 8
ATT_KV_HEADS = 2
ATT_HEAD_DIM = 64
ATT_GROUP = ATT_HEADS // ATT_KV_HEADS
ATT_Q = ATT_HEADS * ATT_HEAD_DIM
ATT_KV = ATT_KV_HEADS * ATT_HEAD_DIM
DN_HEADS = 4
DN_HEAD_DIM = 128
DN_W = DN_HEADS * DN_HEAD_DIM
CONV_WIDTH = 4
RMS_EPS = 1e-6
DN_CHUNK = 128
DN_BASE = 16
GATE_PAD = 128

F32 = jnp.float32
BF16 = jnp.bfloat16

VMEM_LIMIT = 56 * 1024 * 1024


def _resident(shape):
    nd = len(shape)
    return pl.BlockSpec(shape, lambda *_: (0,) * nd, pipeline_mode=pl.Buffered(1))


def _rms(x, w):
    ms = jnp.mean(x * x, axis=-1, keepdims=True)
    return x * lax.rsqrt(ms + RMS_EPS) * w


def _valid_col(rows, tile):
    pos = pl.program_id(1) * tile + lax.broadcasted_iota(jnp.int32, (rows, 1), 0)
    return (pos >= PAD_FRONT).astype(F32)


def _dot(a, b):
    return jnp.dot(a, b, preferred_element_type=F32)


def _dot_nt(a, b):
    return lax.dot_general(a, b, (((1,), (1,)), ((), ())), preferred_element_type=F32)


def _dot_tn(a, b):
    return lax.dot_general(a, b, (((0,), (0,)), ((), ())), preferred_element_type=F32)


_IN_GROUPS = (ATT_Q, 2 * ATT_KV, 3 * DN_W, DN_W, 2 * D_MODEL, GATE_PAD)
IN_MAIN = sum(_IN_GROUPS)


def _inproj_kernel(x_ref, nw_ref, w_ref, b_ref, q_ref, kv_ref, dqkv_ref, dz_ref, gate_ref, dab_ref, *, tile):
    x = x_ref[0]
    hn = _rms(x, nw_ref[...]).astype(BF16)
    valid = _valid_col(tile, tile)
    c0 = 0
    for ref, width in zip((q_ref, kv_ref, dqkv_ref, dz_ref, gate_ref, dab_ref), _IN_GROUPS):
        r = _dot(hn, w_ref[:, c0:c0 + width]) + b_ref[:, c0:c0 + width]
        ref[0] = (r * valid).astype(ref.dtype)
        c0 += width


def _inproj(h, nw, w, b, *, tile):
    B, L, D = h.shape
    row = lambda width: pl.BlockSpec((1, tile, width), lambda i, j: (i, j, 0))
    out_dtypes = (BF16, BF16, BF16, BF16, BF16, F32)
    return pl.pallas_call(
        functools.partial(_inproj_kernel, tile=tile),
        grid=(B, L // tile),
        in_specs=[row(D), _resident((1, D)), _resident((D, IN_MAIN)), _resident((1, IN_MAIN))],
        out_specs=[row(width) for width in _IN_GROUPS],
        out_shape=[jax.ShapeDtypeStruct((B, L, width), dt) for width, dt in zip(_IN_GROUPS, out_dtypes)],
        compiler_params=pltpu.CompilerParams(
            dimension_semantics=("parallel", "parallel"), vmem_limit_bytes=VMEM_LIMIT),
        name="inproj",
    )(h, nw, w, b)


N_KEYS = N_META + 2 * BLOCK


def _attn_kernel(sink_ref, q_ref, kvc_ref, kvp_ref, kvm_ref, o_ref):
    n = pl.program_id(1)
    q = q_ref[0]
    kv = jnp.concatenate([kvm_ref[0, PAD_FRONT:, :], kvp_ref[0], kvc_ref[0]], axis=0)

    rows = ATT_GROUP * BLOCK
    qi = lax.broadcasted_iota(jnp.int32, (rows, N_KEYS), 0) % BLOCK
    kj = lax.broadcasted_iota(jnp.int32, (rows, N_KEYS), 1)
    q_pos = n * BLOCK + qi
    band_pos = (n - 1) * BLOCK + (kj - N_META)
    diff = q_pos - band_pos
    band_ok = (kj >= N_META) & (diff >= 0) & (diff < BLOCK) & (band_pos >= PREFIX)
    meta_ok = (kj < N_META) & ((PAD_FRONT + kj) <= q_pos)
    mask = band_ok | meta_ok

    outs = []
    for g in range(ATT_KV_HEADS):
        k = kv[:, g * ATT_HEAD_DIM:(g + 1) * ATT_HEAD_DIM]
        v = kv[:, ATT_KV + g * ATT_HEAD_DIM:ATT_KV + (g + 1) * ATT_HEAD_DIM]
        heads = [g * ATT_GROUP + i for i in range(ATT_GROUP)]
        qg = jnp.concatenate([q[:, h * ATT_HEAD_DIM:(h + 1) * ATT_HEAD_DIM] for h in heads], axis=0)
        s = _dot_nt(qg, k) * (ATT_HEAD_DIM ** -0.5)
        s = jnp.where(mask, s, -jnp.inf)
        sink = jnp.concatenate([jnp.full((BLOCK, 1), sink_ref[h], F32) for h in heads], axis=0)
        m = jnp.maximum(jnp.max(s, axis=-1, keepdims=True), sink)
        p = jnp.exp(s - m)
        denom = jnp.sum(p, axis=-1, keepdims=True) + jnp.exp(sink - m)
        o = _dot(p.astype(BF16), v) / denom
        outs.extend(o[i * BLOCK:(i + 1) * BLOCK] for i in range(ATT_GROUP))
    o_ref[0] = jnp.concatenate(outs, axis=1).astype(o_ref.dtype)


def _attention(q, kv, sinks):
    B, L, _ = q.shape
    nb = L // BLOCK
    kv_spec = lambda imap: pl.BlockSpec((1, BLOCK, 2 * ATT_KV), imap)
    return pl.pallas_call(
        _attn_kernel,
        grid=(B, nb),
        in_specs=[
            pl.BlockSpec(memory_space=pltpu.SMEM),
            pl.BlockSpec((1, BLOCK, ATT_Q), lambda b, n: (b, n, 0)),
            kv_spec(lambda b, n: (b, n, 0)),
            kv_spec(lambda b, n: (b, jnp.maximum(n - 1, 0), 0)),
            kv_spec(lambda b, n: (b, 0, 0)),
        ],
        out_specs=pl.BlockSpec((1, BLOCK, ATT_Q), lambda b, n: (b, n, 0)),
        out_shape=jax.ShapeDtypeStruct((B, L, ATT_Q), BF16),
        compiler_params=pltpu.CompilerParams(
            dimension_semantics=("parallel", "parallel"), vmem_limit_bytes=VMEM_LIMIT),
        name="attention",
    )(sinks, q, kv, kv, kv)


def _unit_lower_inverse(a):
    C = DN_CHUNK
    ri = lax.broadcasted_iota(jnp.int32, (C, C), 0)
    ci = lax.broadcasted_iota(jnp.int32, (C, C), 1)
    eye = (ri == ci).astype(F32)
    same = lambda size: (ri // size) == (ci // size)

    d = jnp.where(same(DN_BASE), a, 0.0)
    t = eye - d
    d16 = d.astype(BF16)
    power = _dot(d16, d16)
    size = 2
    while size < DN_BASE:
        pb = power.astype(BF16)
        if 2 * size < DN_BASE:
            both = _dot(jnp.concatenate([t.astype(BF16), pb], axis=0), pb)
            t = t + both[:C]
            power = both[C:]
        else:
            t = t + _dot(t.astype(BF16), pb)
        size *= 2
    size = DN_BASE
    while size < C:
        e = jnp.where(same(2 * size) & ~same(size), a, 0.0)
        tb = t.astype(BF16)
        te = _dot(tb, e.astype(BF16))
        t = t - _dot(te.astype(BF16), tb)
        size *= 2
    return t


def _deltanet_kernel(dqkv_ref, dz_ref, dab_ref, convw_ref, alog_ref, dtb_ref, nw_ref, y_ref,
                     ext_ref, state_ref):
    C = DN_CHUNK
    j = pl.program_id(1)

    @pl.when(j == 0)
    def _():
        ext_ref[0:8, :] = jnp.zeros((8, 3 * DN_W), F32)
        state_ref[...] = jnp.zeros_like(state_ref)

    ext_ref[8:, :] = dqkv_ref[0].astype(F32)
    acc = None
    for tap in range(CONV_WIDTH):
        off = 8 - (CONV_WIDTH - 1) + tap
        term = ext_ref[off:off + C, :] * convw_ref[tap:tap + 1, :]
        acc = term if acc is None else acc + term
    ext_ref[0:8, :] = ext_ref[C:C + 8, :]
    qkv = acc * jax.nn.sigmoid(acc)

    dab = dab_ref[0]
    g_all = -jnp.exp(alog_ref[...]) * jax.nn.softplus(dab + dtb_ref[...])
    beta_all = jax.nn.sigmoid(dab)
    ri = lax.broadcasted_iota(jnp.int32, (C, C), 0)
    ci = lax.broadcasted_iota(jnp.int32, (C, C), 1)
    tril = ri >= ci
    strict = ri > ci
    g_hi = g_all.astype(BF16)
    g_lo = (g_all - g_hi.astype(F32)).astype(BF16)
    ones_tril = tril.astype(BF16)
    gcum_all = _dot(ones_tril, g_hi) + _dot(ones_tril, g_lo)
    gcum_rows = gcum_all.T

    outs = []
    for h in range(DN_HEADS):
        lanes = slice(h * DN_HEAD_DIM, (h + 1) * DN_HEAD_DIM)
        q = qkv[:, lanes]
        k = qkv[:, DN_W + h * DN_HEAD_DIM:DN_W + (h + 1) * DN_HEAD_DIM]
        v = qkv[:, 2 * DN_W + h * DN_HEAD_DIM:2 * DN_W + (h + 1) * DN_HEAD_DIM]
        q = q * lax.rsqrt(jnp.sum(q * q, axis=-1, keepdims=True) + RMS_EPS) * (DN_HEAD_DIM ** -0.5)
        k = k * lax.rsqrt(jnp.sum(k * k, axis=-1, keepdims=True) + RMS_EPS)
        gc = gcum_all[:, h:h + 1]
        gr = gcum_rows[h:h + 1, :]
        beta = beta_all[:, DN_HEADS + h:DN_HEADS + h + 1]
        g_last = gcum_all[C - 1:C, h:h + 1]

        decay = jnp.exp(jnp.where(tril, gc - gr, -jnp.inf))
        kb = k * beta
        k16 = k.astype(BF16)
        a = jnp.where(strict, _dot_nt(kb.astype(BF16), k16) * decay, 0.0)
        t = _unit_lower_inverse(a).astype(BF16)
        rhs = jnp.concatenate([kb * jnp.exp(gc), v * beta], axis=1).astype(BF16)
        wu = _dot(t, rhs)
        w = wu[:, :DN_HEAD_DIM]
        u = wu[:, DN_HEAD_DIM:]
        attn = _dot_nt(q.astype(BF16), k16) * decay
        q_dec = q * jnp.exp(gc)
        k_dec = k * jnp.exp(g_last - gc)

        state = state_ref[h]
        s16 = state.astype(BF16)
        v_new = u - _dot(w.astype(BF16), s16)
        vn16 = v_new.astype(BF16)
        o = _dot(q_dec.astype(BF16), s16) + _dot(attn.astype(BF16), vn16)
        state_ref[h] = state * jnp.exp(g_last) + _dot_tn(k_dec.astype(BF16), vn16)

        z = dz_ref[0, :, lanes].astype(F32)
        outs.append(_rms(o, nw_ref[...]) * (z * jax.nn.sigmoid(z)))
    y_ref[0] = jnp.concatenate(outs, axis=1).astype(y_ref.dtype)


def _deltanet(dqkv, dz, dab, conv_w, a_log, dt_bias, norm_w):
    B, L, _ = dqkv.shape
    C = DN_CHUNK
    row = lambda width: pl.BlockSpec((1, C, width), lambda b, j: (b, j, 0))
    return pl.pallas_call(
        _deltanet_kernel,
        grid=(B, L // C),
        in_specs=[row(3 * DN_W), row(DN_W), row(GATE_PAD), _resident((CONV_WIDTH, 3 * DN_W)),
                  _resident((1, GATE_PAD)), _resident((1, GATE_PAD)), _resident((1, DN_HEAD_DIM))],
        out_specs=row(DN_W),
        out_shape=jax.ShapeDtypeStruct((B, L, DN_W), BF16),
        scratch_shapes=[pltpu.VMEM((C + 8, 3 * DN_W), F32),
                        pltpu.VMEM((DN_HEADS, DN_HEAD_DIM, DN_HEAD_DIM), F32)],
        compiler_params=pltpu.CompilerParams(
            dimension_semantics=("parallel", "arbitrary"), vmem_limit_bytes=VMEM_LIMIT),
        name="deltanet",
    )(dqkv, dz, dab, conv_w, a_log, dt_bias, norm_w)


def _mix_ffn_kernel(h_ref, ya_ref, yd_ref, gate_ref, wa_ref, wd_ref, wo_ref, nmix_ref, npre_ref,
                    wfi_ref, wfo_ref, npost_ref, out_ref, *, tile, ff_chunk):
    valid = _valid_col(tile, tile)
    d_ff = wfo_ref.shape[0]
    ga = gate_ref[0, :, :D_MODEL].astype(F32)
    gb = gate_ref[0, :, D_MODEL:].astype(F32)
    merged = (jax.nn.sigmoid(ga) * _dot(ya_ref[0], wa_ref[...])
              + jax.nn.sigmoid(gb) * _dot(yd_ref[0], wd_ref[...]))
    mixed = _dot(merged.astype(BF16), wo_ref[...])
    x = h_ref[0] + valid * _rms(mixed, nmix_ref[...])

    hf = _rms(x, npre_ref[...]).astype(BF16)
    ffn = None
    for c0 in range(0, d_ff, ff_chunk):
        gate = _dot(hf, wfi_ref[:, c0:c0 + ff_chunk])
        up = _dot(hf, wfi_ref[:, d_ff + c0:d_ff + c0 + ff_chunk])
        act = (gate * jax.nn.sigmoid(gate) * up).astype(BF16)
        part = _dot(act, wfo_ref[c0:c0 + ff_chunk, :])
        ffn = part if ffn is None else ffn + part
    out_ref[0] = x + valid * _rms(ffn, npost_ref[...])


def _mix_ffn(h, y_att, y_dn, gates, wa, wd, wo, nmix, npre, wfi, wfo, npost, *, tile, ff_chunk):
    B, L, D = h.shape
    d_ff = wfo.shape[0]
    row = lambda width: pl.BlockSpec((1, tile, width), lambda i, j: (i, j, 0))
    return pl.pallas_call(
        functools.partial(_mix_ffn_kernel, tile=tile, ff_chunk=ff_chunk),
        grid=(B, L // tile),
        in_specs=[row(D), row(ATT_Q), row(DN_W), row(2 * D),
                  _resident((ATT_Q, D)), _resident((DN_W, D)), _resident((D, D)),
                  _resident((1, D)), _resident((1, D)),
                  _resident((D, 2 * d_ff)), _resident((d_ff, D)), _resident((1, D))],
        out_specs=row(D),
        out_shape=jax.ShapeDtypeStruct((B, L, D), F32),
        compiler_params=pltpu.CompilerParams(
            dimension_semantics=("parallel", "parallel"), vmem_limit_bytes=VMEM_LIMIT),
        name="mix_ffn",
    )(h, y_att, y_dn, gates, wa, wd, wo, nmix, npre, wfi, wfo, npost)


def _split_in_proj(w_in, b_in):
    a0 = ATT_Q + 2 * ATT_KV + 4 * DN_W
    g0 = a0 + 2 * DN_HEADS
    pad = GATE_PAD - 2 * DN_HEADS
    w = jnp.concatenate([w_in[:, :a0], w_in[:, g0:], w_in[:, a0:g0],
                         jnp.zeros((w_in.shape[0], pad), w_in.dtype)], axis=1)
    b = jnp.concatenate([b_in[:a0], b_in[g0:], b_in[a0:g0], jnp.zeros((pad,), b_in.dtype)])
    return w.astype(BF16), b.astype(F32)[None, :]


def _pad_row(v, fill=0.0):
    return jnp.concatenate([v.astype(F32), jnp.full((GATE_PAD - v.shape[0],), fill, F32)])[None, :]


def kernel(x, meta_tokens, w_in, b_in, conv_w, a_log, dt_bias, dn_norm_w, att_sinks, w_att_out, w_dn_out,
           w_out, norm_mix_pre, norm_mix_post, norm_ffn_pre, norm_ffn_post, w_ffn_in, w_ffn_out):
    B = x.shape[0]
    depth = w_in.shape[0]
    d_ff = w_ffn_out.shape[1]
    pads = jnp.zeros((B, PAD_FRONT, D_MODEL), x.dtype)
    meta = jnp.broadcast_to(meta_tokens.astype(x.dtype)[None], (B, N_META, D_MODEL))
    h = jnp.concatenate([pads, meta, x], axis=1)
    row = lambda v: v.astype(F32)[None, :]
    for l in range(depth):
        w_main, b_main = _split_in_proj(w_in[l], b_in[l])
        q, kv, dqkv, dz, gates, dab = _inproj(h, row(norm_mix_pre[l]), w_main, b_main, tile=384)
        y_att = _attention(q, kv, att_sinks[l].astype(F32))
        y_dn = _deltanet(dqkv, dz, dab, conv_w[l].astype(F32), _pad_row(a_log[l]), _pad_row(dt_bias[l]),
                         row(dn_norm_w[l]))
        h = _mix_ffn(h, y_att, y_dn, gates, w_att_out[l].astype(BF16), w_dn_out[l].astype(BF16),
                     w_out[l].astype(BF16), row(norm_mix_post[l]), row(norm_ffn_pre[l]),
                     w_ffn_in[l].astype(BF16), w_ffn_out[l].astype(BF16), row(norm_ffn_post[l]),
                     tile=384, ff_chunk=d_ff // 2)
    return h[:, PREFIX:]
```

```python
import functools

import jax
import jax.numpy as jnp
from jax import lax
from jax.experimental import pallas as pl
from jax.experimental.pallas import tpu as pltpu

D_MODEL = 1024
N_META = 16
BLOCK = 128
PREFIX = BLOCK
PAD_FRONT = PREFIX - N_META
ATT_HEADS = 8
ATT_KV_HEADS = 2
ATT_HEAD_DIM = 64
ATT_GROUP = ATT_HEADS // ATT_KV_HEADS
ATT_Q = ATT_HEADS * ATT_HEAD_DIM
ATT_KV = ATT_KV_HEADS * ATT_HEAD_DIM
DN_HEADS = 4
DN_HEAD_DIM = 128
DN_W = DN_HEADS * DN_HEAD_DIM
CONV_WIDTH = 4
RMS_EPS = 1e-6
DN_CHUNK = 128
DN_BASE = 16
LANES = 128
SUBLANES = 8
TILE = 384

PACK_GCUM, PACK_BETA, PACK_EG, PACK_EDEC, PACK_EGL = range(5)

F32 = jnp.float32
BF16 = jnp.bfloat16

VMEM_LIMIT = 56 * 1024 * 1024


def _resident(shape):
    nd = len(shape)
    return pl.BlockSpec(shape, lambda *_: (0,) * nd, pipeline_mode=pl.Buffered(1))


def _rows(width, tile=TILE):
    return pl.BlockSpec((1, tile, width), lambda b, j: (b, j, 0))


def _rms(x, w):
    ms = jnp.mean(x * x, axis=-1, keepdims=True)
    return x * lax.rsqrt(ms + RMS_EPS) * w


def _silu(x):
    return x * jax.nn.sigmoid(x)


def _valid_col(tile):
    pos = pl.program_id(1) * tile + lax.broadcasted_iota(jnp.int32, (tile, 1), 0)
    return (pos >= PAD_FRONT).astype(F32)


def _dot(a, b):
    return jnp.dot(a, b, preferred_element_type=F32)


def _dot_nt(a, b):
    return lax.dot_general(a, b, (((1,), (1,)), ((), ())), preferred_element_type=F32)


def _dot_tn(a, b):
    return lax.dot_general(a, b, (((0,), (0,)), ((), ())), preferred_element_type=F32)


def _iota2(shape, dim):
    return lax.broadcasted_iota(jnp.int32, shape, dim)


_IN_GROUPS = (ATT_Q, 2 * ATT_KV, 3 * DN_W, DN_W, 2 * D_MODEL, LANES, LANES)
IN_MAIN = sum(_IN_GROUPS)


def _inproj_kernel(x_ref, nw_ref, w_ref, b_ref, convw_ref, alog_ref, dtb_ref,
                   q_ref, kv_ref, dq_ref, dk_ref, dv_ref, dz_ref, gate_ref, pack_ref, ext_ref):
    tile = TILE
    j = pl.program_id(1)
    hn = _rms(x_ref[0], nw_ref[...]).astype(BF16)
    valid = _valid_col(tile)
    starts = [sum(_IN_GROUPS[:i]) for i in range(len(_IN_GROUPS))]

    def proj(i):
        c0, width = starts[i], _IN_GROUPS[i]
        return (_dot(hn, w_ref[:, c0:c0 + width]) + b_ref[:, c0:c0 + width]) * valid

    q_ref[0] = (proj(0) * (ATT_HEAD_DIM ** -0.5)).astype(BF16)
    kv_ref[0] = proj(1).astype(BF16)
    dz_ref[0] = proj(3).astype(BF16)
    gate_ref[0] = proj(4).astype(BF16)

    @pl.when(j == 0)
    def _():
        ext_ref[0:SUBLANES, :] = jnp.zeros((SUBLANES, 3 * DN_W), F32)

    ext_ref[SUBLANES:, :] = proj(2)
    acc = None
    for tap in range(CONV_WIDTH):
        off = SUBLANES - (CONV_WIDTH - 1) + tap
        term = ext_ref[off:off + tile, :] * convw_ref[tap:tap + 1, :]
        acc = term if acc is None else acc + term
    ext_ref[0:SUBLANES, :] = ext_ref[tile:tile + SUBLANES, :]
    for h in range(DN_HEADS):
        lanes = slice(h * DN_HEAD_DIM, (h + 1) * DN_HEAD_DIM)
        q = _silu(acc[:, h * DN_HEAD_DIM:(h + 1) * DN_HEAD_DIM])
        k = _silu(acc[:, DN_W + h * DN_HEAD_DIM:DN_W + (h + 1) * DN_HEAD_DIM])
        v = _silu(acc[:, 2 * DN_W + h * DN_HEAD_DIM:2 * DN_W + (h + 1) * DN_HEAD_DIM])
        q = q * (lax.rsqrt(jnp.sum(q * q, axis=-1, keepdims=True) + RMS_EPS) * (DN_HEAD_DIM ** -0.5))
        k = k * lax.rsqrt(jnp.sum(k * k, axis=-1, keepdims=True) + RMS_EPS)
        dq_ref[0, :, lanes] = q.astype(BF16)
        dk_ref[0, :, lanes] = k.astype(BF16)
        dv_ref[0, :, lanes] = v.astype(BF16)

    g = -jnp.exp(alog_ref[...]) * jax.nn.softplus(proj(5) + dtb_ref[...])
    beta = jax.nn.sigmoid(proj(6))
    ri = _iota2((tile, tile), 0)
    ci = _iota2((tile, tile), 1)
    same_chunk = (ri // DN_CHUNK) == (ci // DN_CHUNK)
    sums = jnp.concatenate([(same_chunk & (ri >= ci)).astype(BF16), same_chunk.astype(BF16)], axis=0)
    g_hi = g.astype(BF16)
    g_lo = (g - g_hi.astype(F32)).astype(BF16)
    both = _dot(sums, g_hi) + _dot(sums, g_lo)
    gcum = both[:tile]
    gl = both[tile:]
    grp = _iota2((tile, LANES), 1) // DN_HEADS
    pack = jnp.where(grp == PACK_GCUM, gcum,
                     jnp.where(grp == PACK_BETA, beta,
                               jnp.where(grp == PACK_EG, jnp.exp(gcum),
                                         jnp.where(grp == PACK_EDEC, jnp.exp(gl - gcum), jnp.exp(gl)))))
    pack_ref[0] = pack


def _inproj(h, nw, w, b, conv_w, alog, dtb):
    B, L, D = h.shape
    widths = (ATT_Q, 2 * ATT_KV, DN_W, DN_W, DN_W, DN_W, 2 * D_MODEL, LANES)
    dtypes = (BF16,) * 7 + (F32,)
    return pl.pallas_call(
        _inproj_kernel,
        grid=(B, L // TILE),
        in_specs=[_rows(D), _resident((1, D)), _resident((D, IN_MAIN)), _resident((1, IN_MAIN)),
                  _resident((CONV_WIDTH, 3 * DN_W)), _resident((1, LANES)), _resident((1, LANES))],
        out_specs=[_rows(width) for width in widths],
        out_shape=[jax.ShapeDtypeStruct((B, L, width), dt) for width, dt in zip(widths, dtypes)],
        scratch_shapes=[pltpu.VMEM((TILE + SUBLANES, 3 * DN_W), F32)],
        compiler_params=pltpu.CompilerParams(
            dimension_semantics=("parallel", "arbitrary"), vmem_limit_bytes=VMEM_LIMIT),
        name="inproj",
    )(h, nw, w, b, conv_w, alog, dtb)


ATT_ROWS = ATT_GROUP * BLOCK


def _attn_block(q, kv_prev, kv_cur, kv_meta, fills, *, band, meta_causal):
    qi = _iota2((ATT_ROWS, BLOCK), 0) % BLOCK
    kj = _iota2((ATT_ROWS, BLOCK), 1)
    upper = kj > qi
    meta_ok = kj >= PAD_FRONT
    if meta_causal:
        meta_ok = meta_ok & (kj <= qi)
    ones = jnp.ones((2 * BLOCK, ATT_HEAD_DIM), BF16)

    outs = []
    for g in range(ATT_KV_HEADS):
        ksl = slice(g * ATT_HEAD_DIM, (g + 1) * ATT_HEAD_DIM)
        vsl = slice(ATT_KV + g * ATT_HEAD_DIM, ATT_KV + (g + 1) * ATT_HEAD_DIM)
        heads = [g * ATT_GROUP + i for i in range(ATT_GROUP)]
        qg = jnp.concatenate([q[:, h * ATT_HEAD_DIM:(h + 1) * ATT_HEAD_DIM] for h in heads], axis=0)
        s_meta = jnp.where(meta_ok, _dot_nt(qg, kv_meta[:, ksl]), fills[g])
        if band == "none":
            m = jnp.max(s_meta, axis=-1, keepdims=True)
            p_meta = jnp.exp(s_meta - m).astype(BF16)
            denom = _dot(p_meta, ones[:BLOCK])
            o = _dot(p_meta, kv_meta[:, vsl])
        else:
            s_cur = _dot_nt(qg, kv_cur[:, ksl])
            if band == "full":
                s_band = jnp.where(upper, _dot_nt(qg, kv_prev[:, ksl]), s_cur)
            else:
                s_band = jnp.where(upper, -jnp.inf, s_cur)
            m = jnp.max(jnp.maximum(s_band, s_meta), axis=-1, keepdims=True)
            p_band = jnp.exp(s_band - m)
            p_meta = jnp.exp(s_meta - m).astype(BF16)
            p_cur = jnp.where(upper, 0.0, p_band).astype(BF16)
            if band == "full":
                p_prev = jnp.where(upper, p_band, 0.0).astype(BF16)
                p_all = jnp.concatenate([p_prev, p_cur, p_meta], axis=1)
                v_all = jnp.concatenate([kv_prev[:, vsl], kv_cur[:, vsl], kv_meta[:, vsl]], axis=0)
                o = _dot(p_all, v_all)
                denom = _dot(jnp.concatenate([p_band.astype(BF16), p_meta], axis=1), ones)
            else:
                p_all = jnp.concatenate([p_cur, p_meta], axis=1)
                v_all = jnp.concatenate([kv_cur[:, vsl], kv_meta[:, vsl]], axis=0)
                o = _dot(p_all, v_all)
                denom = _dot(p_all, ones)
        o = o / denom
        outs.extend(o[i * BLOCK:(i + 1) * BLOCK] for i in range(ATT_GROUP))
    return jnp.concatenate(outs, axis=1).astype(BF16)


def _attn_kernel(sink_ref, q_ref, kv_ref, halo_ref, meta_ref, o_ref):
    j = pl.program_id(1)
    nblk = TILE // BLOCK
    kv_meta = meta_ref[0]
    lane = _iota2((BLOCK, BLOCK), 1)
    fills = [jnp.concatenate([jnp.where(lane == 0, sink_ref[g * ATT_GROUP + i], -jnp.inf)
                              for i in range(ATT_GROUP)], axis=0) for g in range(ATT_KV_HEADS)]

    def run(i, band, meta_causal):
        rows = slice(i * BLOCK, (i + 1) * BLOCK)
        kv_prev = halo_ref[0] if i == 0 else kv_ref[0, (i - 1) * BLOCK:i * BLOCK, :]
        o_ref[0, rows, :] = _attn_block(q_ref[0, rows, :], kv_prev, kv_ref[0, rows, :], kv_meta, fills,
                                        band=band, meta_causal=meta_causal)

    @pl.when(j == 0)
    def _():
        run(0, "none", True)
        run(1, "cur", False)
        for i in range(2, nblk):
            run(i, "full", False)

    @pl.when(j > 0)
    def _():
        for i in range(nblk):
            run(i, "full", False)


def _attention(q, kv, sinks):
    B, L, _ = q.shape
    nblk = TILE // BLOCK
    blk = lambda imap: pl.BlockSpec((1, BLOCK, 2 * ATT_KV), imap)
    return pl.pallas_call(
        _attn_kernel,
        grid=(B, L // TILE),
        in_specs=[
            pl.BlockSpec(memory_space=pltpu.SMEM),
            _rows(ATT_Q),
            _rows(2 * ATT_KV),
            blk(lambda b, j: (b, jnp.maximum(j * nblk - 1, 0), 0)),
            blk(lambda b, j: (b, 0, 0)),
        ],
        out_specs=_rows(ATT_Q),
        out_shape=jax.ShapeDtypeStruct((B, L, ATT_Q), BF16),
        compiler_params=pltpu.CompilerParams(
            dimension_semantics=("parallel", "parallel"), vmem_limit_bytes=VMEM_LIMIT),
        name="attention",
    )(sinks, q, kv, kv, kv)


def _inverse_masks():
    C = DN_CHUNK
    ri = _iota2((C, C), 0)
    ci = _iota2((C, C), 1)
    same = lambda size: (ri // size) == (ci // size)
    levels = []
    size = DN_BASE
    while size < C:
        levels.append(same(2 * size) & ~same(size))
        size *= 2
    return (ri == ci).astype(F32), same(DN_BASE), levels


def _unit_lower_inverses(mats, masks):
    C = DN_CHUNK
    eye, base, levels = masks
    ds = [jnp.where(base, a, 0.0) for a in mats]
    ts = [eye - d for d in ds]
    d16 = [d.astype(BF16) for d in ds]
    powers = [_dot(d, d) for d in d16]
    size = 2
    while size < DN_BASE:
        pbs = [p.astype(BF16) for p in powers]
        if 2 * size < DN_BASE:
            both = [_dot(jnp.concatenate([t.astype(BF16), pb], axis=0), pb) for t, pb in zip(ts, pbs)]
            ts = [t + b[:C] for t, b in zip(ts, both)]
            powers = [b[C:] for b in both]
        else:
            ts = [t + _dot(t.astype(BF16), pb) for t, pb in zip(ts, pbs)]
        size *= 2
    for level in levels:
        es = [jnp.where(level, a, 0.0).astype(BF16) for a in mats]
        tbs = [t.astype(BF16) for t in ts]
        tes = [_dot(tb, e).astype(BF16) for tb, e in zip(tbs, es)]
        ts = [t - _dot(te, tb) for t, te, tb in zip(ts, tes, tbs)]
    return ts


def _deltanet_kernel(q_ref, k_ref, v_ref, z_ref, pack_ref, nw_ref, y_ref, state_ref):
    C = DN_CHUNK
    nchunk = TILE // C

    @pl.when(pl.program_id(1) == 0)
    def _():
        state_ref[...] = jnp.zeros_like(state_ref)

    ri = _iota2((C, C), 0)
    ci = _iota2((C, C), 1)
    tril = ri >= ci
    strict = ri > ci
    masks = _inverse_masks()

    chains = [(c, h) for c in range(nchunk) for h in range(DN_HEADS)]
    rows = lambda c: slice(c * C, (c + 1) * C)
    lanes = lambda h: slice(h * DN_HEAD_DIM, (h + 1) * DN_HEAD_DIM)
    packs = [pack_ref[0, rows(c), :] for c in range(nchunk)]
    pack_ts = [p.T for p in packs]
    col = lambda c, h, grp: packs[c][:, DN_HEADS * grp + h:DN_HEADS * grp + h + 1]
    row = lambda c, h, grp: pack_ts[c][DN_HEADS * grp + h:DN_HEADS * grp + h + 1, :]

    q16 = [q_ref[0, rows(c), lanes(h)] for c, h in chains]
    k16 = [k_ref[0, rows(c), lanes(h)] for c, h in chains]
    v16 = [v_ref[0, rows(c), lanes(h)] for c, h in chains]
    decay = [jnp.exp(jnp.where(tril, col(c, h, PACK_GCUM) - row(c, h, PACK_GCUM), -jnp.inf)) for c, h in chains]
    kk = [_dot_nt(k, k) for k in k16]
    qk = [_dot_nt(q, k) for q, k in zip(q16, k16)]
    mats = [jnp.where(strict, x * d, 0.0) * col(c, h, PACK_BETA) for x, d, (c, h) in zip(kk, decay, chains)]
    attn = [(x * d).astype(BF16) for x, d in zip(qk, decay)]
    ts = _unit_lower_inverses(mats, masks)
    tb = [t * row(c, h, PACK_BETA) for t, (c, h) in zip(ts, chains)]
    us = [_dot(t.astype(BF16), v) for t, v in zip(tb, v16)]
    tw = [(t * row(c, h, PACK_EG)).astype(BF16) for t, (c, h) in zip(tb, chains)]
    ws = [_dot(t, k).astype(BF16) for t, k in zip(tw, k16)]
    q_dec = [(q.astype(F32) * col(c, h, PACK_EG)).astype(BF16) for q, (c, h) in zip(q16, chains)]
    k_dec = [(k.astype(F32) * col(c, h, PACK_EDEC)).astype(BF16) for k, (c, h) in zip(k16, chains)]

    for c in range(nchunk):
        idx = [c * DN_HEADS + h for h in range(DN_HEADS)]
        states = [state_ref[h] for h in range(DN_HEADS)]
        s16 = [s.astype(BF16) for s in states]
        both = [_dot(jnp.concatenate([ws[i], q_dec[i]], axis=0), s) for i, s in zip(idx, s16)]
        v_new = [(us[i] - b[:C]).astype(BF16) for i, b in zip(idx, both)]
        outs = [b[C:] + _dot(attn[i], vn) for i, b, vn in zip(idx, both, v_new)]
        for h, (i, s, vn) in enumerate(zip(idx, states, v_new)):
            egl = packs[c][C - 1:C, DN_HEADS * PACK_EGL + h:DN_HEADS * PACK_EGL + h + 1]
            state_ref[h] = s * egl + _dot_tn(k_dec[i], vn)
        for h, o in enumerate(outs):
            z = z_ref[0, rows(c), lanes(h)].astype(F32)
            y_ref[0, rows(c), lanes(h)] = (_rms(o, nw_ref[...]) * _silu(z)).astype(BF16)


def _deltanet(dq, dk, dv, dz, pack, norm_w):
    B, L, _ = dq.shape
    return pl.pallas_call(
        _deltanet_kernel,
        grid=(B, L // TILE),
        in_specs=[_rows(DN_W), _rows(DN_W), _rows(DN_W), _rows(DN_W), _rows(LANES),
                  _resident((1, DN_HEAD_DIM))],
        out_specs=_rows(DN_W),
        out_shape=jax.ShapeDtypeStruct((B, L, DN_W), BF16),
        scratch_shapes=[pltpu.VMEM((DN_HEADS, DN_HEAD_DIM, DN_HEAD_DIM), F32)],
        compiler_params=pltpu.CompilerParams(
            dimension_semantics=("parallel", "arbitrary"), vmem_limit_bytes=VMEM_LIMIT),
        name="deltanet",
    )(dq, dk, dv, dz, pack, norm_w)


def _mix_ffn_kernel(h_ref, ya_ref, yd_ref, gate_ref, wa_ref, wd_ref, wo_ref, nmix_ref, npre_ref,
                    wfi_ref, wfo_ref, npost_ref, out_ref, *, ff_chunk):
    valid = _valid_col(TILE)
    d_ff = wfo_ref.shape[0]
    ga = gate_ref[0, :, :D_MODEL].astype(F32)
    gb = gate_ref[0, :, D_MODEL:].astype(F32)
    merged = (jax.nn.sigmoid(ga) * _dot(ya_ref[0], wa_ref[...])
              + jax.nn.sigmoid(gb) * _dot(yd_ref[0], wd_ref[...]))
    mixed = _dot(merged.astype(BF16), wo_ref[...])
    x = h_ref[0] + valid * _rms(mixed, nmix_ref[...])

    hf = _rms(x, npre_ref[...]).astype(BF16)
    ffn = None
    for c0 in range(0, d_ff, ff_chunk):
        gate = _dot(hf, wfi_ref[:, c0:c0 + ff_chunk])
        up = _dot(hf, wfi_ref[:, d_ff + c0:d_ff + c0 + ff_chunk])
        act = (_silu(gate) * up).astype(BF16)
        part = _dot(act, wfo_ref[c0:c0 + ff_chunk, :])
        ffn = part if ffn is None else ffn + part
    out_ref[0] = x + valid * _rms(ffn, npost_ref[...])


def _mix_ffn(h, y_att, y_dn, gates, wa, wd, wo, nmix, npre, wfi, wfo, npost, *, ff_chunk):
    B, L, D = h.shape
    d_ff = wfo.shape[0]
    return pl.pallas_call(
        functools.partial(_mix_ffn_kernel, ff_chunk=ff_chunk),
        grid=(B, L // TILE),
        in_specs=[_rows(D), _rows(ATT_Q), _rows(DN_W), _rows(2 * D),
                  _resident((ATT_Q, D)), _resident((DN_W, D)), _resident((D, D)),
                  _resident((1, D)), _resident((1, D)),
                  _resident((D, 2 * d_ff)), _resident((d_ff, D)), _resident((1, D))],
        out_specs=_rows(D),
        out_shape=jax.ShapeDtypeStruct((B, L, D), F32),
        compiler_params=pltpu.CompilerParams(
            dimension_semantics=("parallel", "parallel"), vmem_limit_bytes=VMEM_LIMIT),
        name="mix_ffn",
    )(h, y_att, y_dn, gates, wa, wd, wo, nmix, npre, wfi, wfo, npost)


def _split_in_proj(w_in, b_in):
    a0 = ATT_Q + 2 * ATT_KV + 4 * DN_W
    b0 = a0 + DN_HEADS
    g0 = b0 + DN_HEADS
    reps = LANES // DN_HEADS
    w = jnp.concatenate([w_in[:, :a0], w_in[:, g0:], jnp.tile(w_in[:, a0:b0], (1, reps)),
                         jnp.tile(w_in[:, b0:g0], (1, reps))], axis=1)
    b = jnp.concatenate([b_in[:a0], b_in[g0:], jnp.tile(b_in[a0:b0], reps), jnp.tile(b_in[b0:g0], reps)])
    return w.astype(BF16), b.astype(F32)[None, :]


def kernel(x, meta_tokens, w_in, b_in, conv_w, a_log, dt_bias, dn_norm_w, att_sinks, w_att_out, w_dn_out,
           w_out, norm_mix_pre, norm_mix_post, norm_ffn_pre, norm_ffn_post, w_ffn_in, w_ffn_out):
    B = x.shape[0]
    depth = w_in.shape[0]
    d_ff = w_ffn_out.shape[1]
    pads = jnp.zeros((B, PAD_FRONT, D_MODEL), x.dtype)
    meta = jnp.broadcast_to(meta_tokens.astype(x.dtype)[None], (B, N_META, D_MODEL))
    h = jnp.concatenate([pads, meta, x], axis=1)
    row = lambda v: v.astype(F32)[None, :]
    lane_tiled = lambda v: jnp.tile(v.astype(F32), LANES // DN_HEADS)[None, :]
    for l in range(depth):
        w_main, b_main = _split_in_proj(w_in[l], b_in[l])
        q, kv, dq, dk, dv, dz, gates, pack = _inproj(h, row(norm_mix_pre[l]), w_main, b_main,
                                                     conv_w[l].astype(F32), lane_tiled(a_log[l]),
                                                     lane_tiled(dt_bias[l]))
        y_att = _attention(q, kv, att_sinks[l].astype(F32))
        y_dn = _deltanet(dq, dk, dv, dz, pack, row(dn_norm_w[l]))
        h = _mix_ffn(h, y_att, y_dn, gates, w_att_out[l].astype(BF16), w_dn_out[l].astype(BF16),
                     w_out[l].astype(BF16), row(norm_mix_post[l]), row(norm_ffn_pre[l]),
                     w_ffn_in[l].astype(BF16), w_ffn_out[l].astype(BF16), row(norm_ffn_post[l]),
                     ff_chunk=d_ff // 2)
    return h[:, PREFIX:]
```

```python
import jax
import jax.numpy as jnp
from jax import lax
from jax.experimental import pallas as pl
from jax.experimental.pallas import tpu as pltpu

D_MODEL = 1024
N_META = 16
BLOCK = 128
PREFIX = BLOCK
PAD_FRONT = PREFIX - N_META
ATT_HEADS = 8
ATT_KV_HEADS = 2
ATT_HEAD_DIM = 64
ATT_GROUP = ATT_HEADS // ATT_KV_HEADS
ATT_Q = ATT_HEADS * ATT_HEAD_DIM
ATT_KV = ATT_KV_HEADS * ATT_HEAD_DIM
DN_HEADS = 4
DN_HEAD_DIM = 128
DN_W = DN_HEADS * DN_HEAD_DIM
CONV_WIDTH = 4
RMS_EPS = 1e-6
DN_CHUNK = 128
DN_BASE = 16
LANES = 128
SUBLANES = 8
TILE = 384

PACK_GCUM, PACK_BETA, PACK_EG, PACK_EDEC, PACK_EGL = range(5)

F32 = jnp.float32
BF16 = jnp.bfloat16

VMEM_LIMIT = 56 * 1024 * 1024


def _resident(shape):
    nd = len(shape)
    return pl.BlockSpec(shape, lambda *_: (0,) * nd, pipeline_mode=pl.Buffered(1))


def _rows(width, tile=TILE):
    return pl.BlockSpec((1, tile, width), lambda b, j: (b, j, 0))


def _rms(x, w):
    ms = jnp.mean(x * x, axis=-1, keepdims=True)
    return x * lax.rsqrt(ms + RMS_EPS) * w


def _silu(x):
    return x * jax.nn.sigmoid(x)


def _dot(a, b):
    return jnp.dot(a, b, preferred_element_type=F32)


def _dot_nt(a, b):
    return lax.dot_general(a, b, (((1,), (1,)), ((), ())), preferred_element_type=F32)


def _dot_tn(a, b):
    return lax.dot_general(a, b, (((0,), (0,)), ((), ())), preferred_element_type=F32)


def _iota2(shape, dim):
    return lax.broadcasted_iota(jnp.int32, shape, dim)


IN_DQKV = 0
IN_Q = IN_DQKV + 3 * DN_W
IN_KV = IN_Q + ATT_Q
IN_DA = IN_KV + 2 * ATT_KV
IN_DB = IN_DA + LANES
IN_DZ = IN_DB + LANES
IN_GATE = IN_DZ + DN_W
IN_MAIN = IN_GATE + 2 * D_MODEL + LANES
IN_PIECE = 4 * LANES


def _mask_pads(r, keep):
    return jnp.concatenate([r[:PAD_FRONT] * keep, r[PAD_FRONT:]], axis=0)


def _inproj_kernel(x_ref, nw_ref, w_ref, b_ref, convw_ref, alog_ref, dtb_ref,
                   q_ref, kv_ref, dq_ref, dk_ref, dv_ref, dz_ref, gate_ref, pack_ref, ext_ref):
    tile = TILE
    j = pl.program_id(1)
    keep = jnp.where(j == 0, 0.0, 1.0).astype(F32)

    @pl.when(j == 0)
    def _():
        ext_ref[0:SUBLANES, :] = jnp.zeros((SUBLANES, 3 * DN_W), F32)

    hn = _rms(x_ref[0], nw_ref[...]).astype(BF16)

    def proj(c0):
        cols = slice(c0, c0 + IN_PIECE)
        return _mask_pads(_dot(hn, w_ref[:, cols]) + b_ref[:, cols], keep)

    def after(v):
        bits = pltpu.bitcast(v[:SUBLANES, :LANES], jnp.uint32)
        return pltpu.bitcast(lax.shift_right_logical(bits, jnp.uint32(32)), F32)[:1]

    def conv_group(i, anchor):
        cols = slice(i * DN_HEAD_DIM, (i + 1) * DN_HEAD_DIM)
        x = ext_ref[:, cols]
        taps = [convw_ref[t:t + 1, cols] + anchor for t in range(CONV_WIDTH)]
        acc = x[SUBLANES:] * taps[CONV_WIDTH - 1]
        for shift in range(1, CONV_WIDTH):
            acc = acc + pltpu.roll(x, shift, 0)[SUBLANES:] * taps[CONV_WIDTH - 1 - shift]
        y = _silu(acc)
        kind, head = divmod(i, DN_HEADS)
        lanes = slice(head * DN_HEAD_DIM, (head + 1) * DN_HEAD_DIM)
        if kind == 2:
            dv_ref[0, :, lanes] = y.astype(BF16)
        else:
            scale = lax.rsqrt(jnp.sum(y * y, axis=-1, keepdims=True) + RMS_EPS)
            if kind == 0:
                dq_ref[0, :, lanes] = (y * (scale * (DN_HEAD_DIM ** -0.5))).astype(BF16)
            else:
                dk_ref[0, :, lanes] = (y * scale).astype(BF16)

    n_conv = (1, 1, 2, 1, 1, 2, 1, 1, 2)
    piece_starts = ([IN_DQKV + c0 for c0 in range(0, 3 * DN_W, IN_PIECE)] + [IN_Q, IN_KV, IN_DZ]
                    + [IN_GATE + c0 for c0 in range(0, 2 * D_MODEL, IN_PIECE)])
    group = 0
    prev = None
    kv_logits = None
    for n, c0 in enumerate(piece_starts):
        r = proj(c0)
        if c0 < IN_Q:
            ext_ref[SUBLANES:, c0 - IN_DQKV:c0 - IN_DQKV + IN_PIECE] = r
        elif c0 == IN_Q:
            q_ref[0] = (r * (ATT_HEAD_DIM ** -0.5)).astype(BF16)
        elif c0 == IN_KV:
            kv_logits = r
            kv_ref[0] = r[:, :2 * ATT_KV].astype(BF16)
        elif c0 == IN_DZ:
            dz_ref[0] = r.astype(BF16)
        else:
            gate_ref[0, :, c0 - IN_GATE:c0 - IN_GATE + IN_PIECE] = r.astype(BF16)
        if n >= 1:
            for _ in range(n_conv[n - 1]):
                conv_group(group, after(prev))
                group += 1
        prev = r
    ext_ref[0:SUBLANES, :] = ext_ref[tile:tile + SUBLANES, :]

    da = kv_logits[:, IN_DA - IN_KV:IN_DB - IN_KV]
    db = kv_logits[:, IN_DB - IN_KV:IN_DZ - IN_KV]
    g = -jnp.exp(alog_ref[...]) * jax.nn.softplus(da + dtb_ref[...])
    beta = jax.nn.sigmoid(db)
    ri = _iota2((tile, tile), 0)
    ci = _iota2((tile, tile), 1)
    same_chunk = (ri // DN_CHUNK) == (ci // DN_CHUNK)
    sums = jnp.concatenate([(same_chunk & (ri >= ci)).astype(BF16), same_chunk.astype(BF16)], axis=0)
    g_hi = g.astype(BF16)
    g_lo = (g - g_hi.astype(F32)).astype(BF16)
    both = _dot(sums, g_hi) + _dot(sums, g_lo)
    gcum = both[:tile]
    gl = both[tile:]
    grp = _iota2((tile, LANES), 1) // DN_HEADS
    pack = jnp.where(grp == PACK_GCUM, gcum,
                     jnp.where(grp == PACK_BETA, beta,
                               jnp.where(grp == PACK_EG, jnp.exp(gcum),
                                         jnp.where(grp == PACK_EDEC, jnp.exp(gl - gcum), jnp.exp(gl)))))
    pack_ref[0] = pack


def _inproj(h, nw, w, b, conv_w, alog, dtb):
    B, L, D = h.shape
    widths = (ATT_Q, 2 * ATT_KV, DN_W, DN_W, DN_W, DN_W, 2 * D_MODEL, LANES)
    dtypes = (BF16,) * 7 + (F32,)
    return pl.pallas_call(
        _inproj_kernel,
        grid=(B, L // TILE),
        in_specs=[_rows(D), _resident((1, D)), _resident((D, IN_MAIN)), _resident((1, IN_MAIN)),
                  _resident((CONV_WIDTH, 3 * DN_W)), _resident((1, LANES)), _resident((1, LANES))],
        out_specs=[_rows(width) for width in widths],
        out_shape=[jax.ShapeDtypeStruct((B, L, width), dt) for width, dt in zip(widths, dtypes)],
        scratch_shapes=[pltpu.VMEM((TILE + SUBLANES, 3 * DN_W), F32)],
        compiler_params=pltpu.CompilerParams(
            dimension_semantics=("parallel", "arbitrary"), vmem_limit_bytes=VMEM_LIMIT),
        name="inproj",
    )(h, nw, w, b, conv_w, alog, dtb)


ATT_ROWS = ATT_GROUP * BLOCK


def _attn_blocks(blocks, kv_meta, fills):
    qi = _iota2((ATT_ROWS, BLOCK), 0) % BLOCK
    kj = _iota2((ATT_ROWS, BLOCK), 1)
    upper = kj > qi
    meta_lanes = kj >= PAD_FRONT
    meta_lanes_causal = meta_lanes & (kj <= qi)

    chains = [(blk, g) for blk in blocks for g in range(ATT_KV_HEADS)]
    ksl = lambda g: slice(g * ATT_HEAD_DIM, (g + 1) * ATT_HEAD_DIM)
    vsl = lambda g: slice(ATT_KV + g * ATT_HEAD_DIM, ATT_KV + (g + 1) * ATT_HEAD_DIM)

    def sources(blk):
        _, kv_prev, kv_cur, band, _ = blk
        return {"full": [kv_prev, kv_cur, kv_meta], "cur": [kv_cur, kv_meta], "none": [kv_meta]}[band]

    s_all = []
    for blk, g in chains:
        q = blk[0]
        qg = jnp.concatenate([q[:, h * ATT_HEAD_DIM:(h + 1) * ATT_HEAD_DIM]
                              for h in range(g * ATT_GROUP, (g + 1) * ATT_GROUP)], axis=0)
        s_all.append(_dot_nt(qg, jnp.concatenate([s[:, ksl(g)] for s in sources(blk)], axis=0)))
    v_ext = []
    for blk, g in chains:
        src = sources(blk)
        ones = jnp.ones((len(src) * BLOCK, ATT_HEAD_DIM), BF16)
        v_ext.append(jnp.concatenate([jnp.concatenate([s[:, vsl(g)] for s in src], axis=0), ones], axis=1))

    s_band, s_meta = [], []
    for (blk, g), s in zip(chains, s_all):
        band, meta_causal = blk[3], blk[4]
        tiles = [s[:, i * BLOCK:(i + 1) * BLOCK] for i in range(s.shape[1] // BLOCK)]
        s_meta.append(jnp.where(meta_lanes_causal if meta_causal else meta_lanes, tiles[-1], fills[g]))
        if band == "full":
            s_band.append(jnp.where(upper, tiles[0], tiles[1]))
        elif band == "cur":
            s_band.append(jnp.where(upper, -jnp.inf, tiles[0]))
        else:
            s_band.append(None)
    m = [jnp.max(sm if sb is None else jnp.maximum(sb, sm), axis=-1, keepdims=True)
         for sb, sm in zip(s_band, s_meta)]
    p_all = []
    for (blk, g), sb, sm, mx in zip(chains, s_band, s_meta, m):
        p_meta = jnp.exp(sm - mx).astype(BF16)
        if sb is None:
            p_all.append(p_meta)
            continue
        p_band = jnp.exp(sb - mx)
        if blk[3] == "full":
            p_all.append(jnp.concatenate([jnp.where(upper, p_band, 0.0).astype(BF16),
                                          jnp.where(upper, 0.0, p_band).astype(BF16), p_meta], axis=1))
        else:
            p_all.append(jnp.concatenate([p_band.astype(BF16), p_meta], axis=1))
    o_ext = [_dot(p, v) for p, v in zip(p_all, v_ext)]
    o = [(x * pltpu.roll(1.0 / x, ATT_HEAD_DIM, 1))[:, :ATT_HEAD_DIM] for x in o_ext]

    outs = []
    for b in range(len(blocks)):
        heads = [o[b * ATT_KV_HEADS + g][i * BLOCK:(i + 1) * BLOCK]
                 for g in range(ATT_KV_HEADS) for i in range(ATT_GROUP)]
        outs.append(jnp.concatenate(heads, axis=1).astype(BF16))
    return outs


def _attn_kernel(sink_ref, q_ref, kv_ref, halo_ref, meta_ref, o_ref):
    j = pl.program_id(1)
    nblk = TILE // BLOCK
    kv_meta = meta_ref[0]
    lane = _iota2((BLOCK, BLOCK), 1)
    fills = [jnp.concatenate([jnp.where(lane == 0, sink_ref[g * ATT_GROUP + i], -jnp.inf)
                              for i in range(ATT_GROUP)], axis=0) for g in range(ATT_KV_HEADS)]
    rows = lambda i: slice(i * BLOCK, (i + 1) * BLOCK)

    def run(modes):
        blocks = [(q_ref[0, rows(i), :], halo_ref[0] if i == 0 else kv_ref[0, rows(i - 1), :],
                   kv_ref[0, rows(i), :], band, meta_causal) for i, (band, meta_causal) in enumerate(modes)]
        for i, out in enumerate(_attn_blocks(blocks, kv_meta, fills)):
            o_ref[0, rows(i), :] = out

    @pl.when(j == 0)
    def _():
        run([("none", True), ("cur", False)] + [("full", False)] * (nblk - 2))

    @pl.when(j > 0)
    def _():
        run([("full", False)] * nblk)


def _attention(q, kv, sinks):
    B, L, _ = q.shape
    nblk = TILE // BLOCK
    blk = lambda imap: pl.BlockSpec((1, BLOCK, 2 * ATT_KV), imap)
    return pl.pallas_call(
        _attn_kernel,
        grid=(B, L // TILE),
        in_specs=[
            pl.BlockSpec(memory_space=pltpu.SMEM),
            _rows(ATT_Q),
            _rows(2 * ATT_KV),
            blk(lambda b, j: (b, jnp.maximum(j * nblk - 1, 0), 0)),
            blk(lambda b, j: (b, 0, 0)),
        ],
        out_specs=_rows(ATT_Q),
        out_shape=jax.ShapeDtypeStruct((B, L, ATT_Q), BF16),
        compiler_params=pltpu.CompilerParams(
            dimension_semantics=("parallel", "parallel"), vmem_limit_bytes=VMEM_LIMIT),
        name="attention",
    )(sinks, q, kv, kv, kv)


def _inverse_masks():
    C = DN_CHUNK
    ri = _iota2((C, C), 0)
    ci = _iota2((C, C), 1)
    same = lambda size: (ri // size) == (ci // size)
    levels = []
    size = DN_BASE
    while size < C:
        levels.append(same(2 * size) & ~same(size))
        size *= 2
    return (ri == ci).astype(F32), same(DN_BASE), levels


def _unit_lower_inverses(mats, masks):
    C = DN_CHUNK
    eye, base, levels = masks
    ds = [jnp.where(base, a, 0.0) for a in mats]
    ts = [eye - d for d in ds]
    d16 = [d.astype(BF16) for d in ds]
    powers = [_dot(d, d) for d in d16]
    size = 2
    while size < DN_BASE:
        pbs = [p.astype(BF16) for p in powers]
        if 2 * size < DN_BASE:
            both = [_dot(jnp.concatenate([t.astype(BF16), pb], axis=0), pb) for t, pb in zip(ts, pbs)]
            ts = [t + b[:C] for t, b in zip(ts, both)]
            powers = [b[C:] for b in both]
        else:
            ts = [t + _dot(t.astype(BF16), pb) for t, pb in zip(ts, pbs)]
        size *= 2
    for level in levels:
        es = [jnp.where(level, a, 0.0).astype(BF16) for a in mats]
        tbs = [t.astype(BF16) for t in ts]
        tes = [_dot(tb, e).astype(BF16) for tb, e in zip(tbs, es)]
        ts = [t - _dot(te, tb) for t, te, tb in zip(ts, tes, tbs)]
    return ts


def _deltanet_kernel(q_ref, k_ref, v_ref, z_ref, pack_ref, nw_ref, y_ref, state_ref):
    C = DN_CHUNK
    nchunk = TILE // C

    @pl.when(pl.program_id(1) == 0)
    def _():
        state_ref[...] = jnp.zeros_like(state_ref)

    ri = _iota2((C, C), 0)
    ci = _iota2((C, C), 1)
    tril = ri >= ci
    strict = ri > ci
    masks = _inverse_masks()

    chains = [(c, h) for c in range(nchunk) for h in range(DN_HEADS)]
    rows = lambda c: slice(c * C, (c + 1) * C)
    lanes = lambda h: slice(h * DN_HEAD_DIM, (h + 1) * DN_HEAD_DIM)
    packs = [pack_ref[0, rows(c), :] for c in range(nchunk)]
    pack_ts = [p.T for p in packs]
    col = lambda c, h, grp: packs[c][:, DN_HEADS * grp + h:DN_HEADS * grp + h + 1]
    row = lambda c, h, grp: pack_ts[c][DN_HEADS * grp + h:DN_HEADS * grp + h + 1, :]

    q16 = [q_ref[0, rows(c), lanes(h)] for c, h in chains]
    k16 = [k_ref[0, rows(c), lanes(h)] for c, h in chains]
    v16 = [v_ref[0, rows(c), lanes(h)] for c, h in chains]
    decay = [jnp.exp(jnp.where(tril, col(c, h, PACK_GCUM) - row(c, h, PACK_GCUM), -jnp.inf)) for c, h in chains]
    kk = [_dot_nt(k, k) for k in k16]
    qk = [_dot_nt(q, k) for q, k in zip(q16, k16)]
    mats = [jnp.where(strict, x * d, 0.0) * col(c, h, PACK_BETA) for x, d, (c, h) in zip(kk, decay, chains)]
    attn = [(x * d).astype(BF16) for x, d in zip(qk, decay)]
    ts = _unit_lower_inverses(mats, masks)
    tb = [t * row(c, h, PACK_BETA) for t, (c, h) in zip(ts, chains)]
    us = [_dot(t.astype(BF16), v) for t, v in zip(tb, v16)]
    tw = [(t * row(c, h, PACK_EG)).astype(BF16) for t, (c, h) in zip(tb, chains)]
    ws = [_dot(t, k).astype(BF16) for t, k in zip(tw, k16)]
    q_dec = [(q.astype(F32) * col(c, h, PACK_EG)).astype(BF16) for q, (c, h) in zip(q16, chains)]
    k_dec = [(k.astype(F32) * col(c, h, PACK_EDEC)).astype(BF16) for k, (c, h) in zip(k16, chains)]

    for c in range(nchunk):
        idx = [c * DN_HEADS + h for h in range(DN_HEADS)]
        states = [state_ref[h] for h in range(DN_HEADS)]
        s16 = [s.astype(BF16) for s in states]
        both = [_dot(jnp.concatenate([ws[i], q_dec[i]], axis=0), s) for i, s in zip(idx, s16)]
        v_new = [(us[i] - b[:C]).astype(BF16) for i, b in zip(idx, both)]
        outs = [b[C:] + _dot(attn[i], vn) for i, b, vn in zip(idx, both, v_new)]
        for h, (i, s, vn) in enumerate(zip(idx, states, v_new)):
            egl = packs[c][C - 1:C, DN_HEADS * PACK_EGL + h:DN_HEADS * PACK_EGL + h + 1]
            state_ref[h] = s * egl + _dot_tn(k_dec[i], vn)
        for h, o in enumerate(outs):
            z = z_ref[0, rows(c), lanes(h)].astype(F32)
            y_ref[0, rows(c), lanes(h)] = (_rms(o, nw_ref[...]) * _silu(z)).astype(BF16)


def _deltanet(dq, dk, dv, dz, pack, norm_w):
    B, L, _ = dq.shape
    return pl.pallas_call(
        _deltanet_kernel,
        grid=(B, L // TILE),
        in_specs=[_rows(DN_W), _rows(DN_W), _rows(DN_W), _rows(DN_W), _rows(LANES),
                  _resident((1, DN_HEAD_DIM))],
        out_specs=_rows(DN_W),
        out_shape=jax.ShapeDtypeStruct((B, L, DN_W), BF16),
        scratch_shapes=[pltpu.VMEM((DN_HEADS, DN_HEAD_DIM, DN_HEAD_DIM), F32)],
        compiler_params=pltpu.CompilerParams(
            dimension_semantics=("parallel", "arbitrary"), vmem_limit_bytes=VMEM_LIMIT),
        name="deltanet",
    )(dq, dk, dv, dz, pack, norm_w)


def _mix_ffn_kernel(h_ref, ya_ref, yd_ref, gate_ref, wdown_ref, nmix_ref, npre_ref, wfi_ref, npost_ref, out_ref):
    keep = jnp.where(pl.program_id(1) == 0, 0.0, 1.0).astype(F32)
    d_ff = wfi_ref.shape[1] // 2
    r_att, r_dn, r_out = 0, ATT_Q, ATT_Q + DN_W
    r_ffn = r_out + D_MODEL
    wdown = lambda r0, r1: wdown_ref[r0:r1, :D_MODEL]

    ga = gate_ref[0, :, :D_MODEL].astype(F32)
    gb = gate_ref[0, :, D_MODEL:].astype(F32)
    merged = (jax.nn.sigmoid(ga) * _dot(ya_ref[0], wdown(r_att, r_dn))
              + jax.nn.sigmoid(gb) * _dot(yd_ref[0], wdown(r_dn, r_out)))
    mixed = _dot(merged.astype(BF16), wdown(r_out, r_ffn))
    x = h_ref[0] + _mask_pads(_rms(mixed, nmix_ref[...]), keep)

    hf = _rms(x, npre_ref[...]).astype(BF16)
    gate = _dot(hf, wfi_ref[:, :d_ff])
    up = _dot(hf, wfi_ref[:, d_ff:])
    ffn = _dot((_silu(gate) * up).astype(BF16), wdown(r_ffn, r_ffn + d_ff))
    out_ref[0] = x + _mask_pads(_rms(ffn, npost_ref[...]), keep)


def _mix_ffn(h, y_att, y_dn, gates, wdown, nmix, npre, wfi, npost):
    B, L, D = h.shape
    return pl.pallas_call(
        _mix_ffn_kernel,
        grid=(B, L // TILE),
        in_specs=[_rows(D), _rows(ATT_Q), _rows(DN_W), _rows(2 * D),
                  _resident(wdown.shape), _resident((1, D)), _resident((1, D)),
                  _resident(wfi.shape), _resident((1, D))],
        out_specs=_rows(D),
        out_shape=jax.ShapeDtypeStruct((B, L, D), F32),
        compiler_params=pltpu.CompilerParams(
            dimension_semantics=("parallel", "parallel"), vmem_limit_bytes=VMEM_LIMIT),
        name="mix_ffn",
    )(h, y_att, y_dn, gates, wdown, nmix, npre, wfi, npost)


def _split_in_proj(w_in, b_in):
    d0 = ATT_Q + 2 * ATT_KV
    z0 = d0 + 3 * DN_W
    a0 = z0 + DN_W
    b0 = a0 + DN_HEADS
    g0 = b0 + DN_HEADS
    reps = LANES // DN_HEADS

    def regroup(t):
        return jnp.concatenate([t[..., d0:z0], t[..., :d0], jnp.tile(t[..., a0:b0], reps),
                                jnp.tile(t[..., b0:g0], reps), t[..., z0:a0], t[..., g0:],
                                jnp.zeros(t.shape[:-1] + (LANES,), t.dtype)], axis=-1)

    return regroup(w_in).astype(BF16), regroup(b_in).astype(F32)[None, :]


def kernel(x, meta_tokens, w_in, b_in, conv_w, a_log, dt_bias, dn_norm_w, att_sinks, w_att_out, w_dn_out,
           w_out, norm_mix_pre, norm_mix_post, norm_ffn_pre, norm_ffn_post, w_ffn_in, w_ffn_out):
    B = x.shape[0]
    depth = w_in.shape[0]
    d_ff = w_ffn_out.shape[1]
    pads = jnp.zeros((B, PAD_FRONT, D_MODEL), x.dtype)
    meta = jnp.broadcast_to(meta_tokens.astype(x.dtype)[None], (B, N_META, D_MODEL))
    h = jnp.concatenate([pads, meta, x], axis=1)
    row = lambda v: v.astype(F32)[None, :]
    lane_tiled = lambda v: jnp.tile(v.astype(F32), LANES // DN_HEADS)[None, :]
    for l in range(depth):
        w_main, b_main = _split_in_proj(w_in[l], b_in[l])
        q, kv, dq, dk, dv, dz, gates, pack = _inproj(h, row(norm_mix_pre[l]), w_main, b_main,
                                                     conv_w[l].astype(F32), lane_tiled(a_log[l]),
                                                     lane_tiled(dt_bias[l]))
        y_att = _attention(q, kv, att_sinks[l].astype(F32))
        y_dn = _deltanet(dq, dk, dv, dz, pack, row(dn_norm_w[l]))
        wdown = jnp.concatenate([w_att_out[l], w_dn_out[l], w_out[l], w_ffn_out[l]], axis=0).astype(BF16)
        wdown = jnp.pad(wdown, ((0, 0), (0, LANES)))
        h = _mix_ffn(h, y_att, y_dn, gates, wdown, row(norm_mix_post[l]), row(norm_ffn_pre[l]),
                     w_ffn_in[l].astype(BF16), row(norm_ffn_post[l]))
    return h[:, PREFIX:]
```

```python
import jax
import jax.numpy as jnp
from jax import lax
from jax.experimental import pallas as pl
from jax.experimental.pallas import tpu as pltpu

D_MODEL = 1024
N_META = 16
BLOCK = 128
PREFIX = BLOCK
PAD_FRONT = PREFIX - N_META
ATT_HEADS = 8
ATT_KV_HEADS = 2
ATT_HEAD_DIM = 64
ATT_GROUP = ATT_HEADS // ATT_KV_HEADS
ATT_Q = ATT_HEADS * ATT_HEAD_DIM
ATT_KV = ATT_KV_HEADS * ATT_HEAD_DIM
DN_HEADS = 4
DN_HEAD_DIM = 128
DN_W = DN_HEADS * DN_HEAD_DIM
CONV_WIDTH = 4
RMS_EPS = 1e-6
DN_CHUNK = 128
DN_BASE = 16
LANES = 128
SUBLANES = 8
TILE = 384

PACK_GCUM, PACK_BETA, PACK_EG, PACK_EDEC, PACK_EGL = range(5)

F32 = jnp.float32
BF16 = jnp.bfloat16

VMEM_LIMIT = 56 * 1024 * 1024


def _resident(shape, layer=None):
    if layer is None:
        return pl.BlockSpec(shape, lambda *_: (0,) * len(shape), pipeline_mode=pl.Buffered(1))
    nd = len(shape) - 1
    return pl.BlockSpec((None,) + tuple(shape[1:]), lambda *_: (layer,) + (0,) * nd,
                        pipeline_mode=pl.Buffered(1))


def _rows(width, tile=TILE):
    return pl.BlockSpec((1, tile, width), lambda b, j: (b, j, 0))


def _rms(x, w):
    ms = jnp.mean(x * x, axis=-1, keepdims=True)
    return x * lax.rsqrt(ms + RMS_EPS) * w


def _silu(x):
    return x * jax.nn.sigmoid(x)


def _dot(a, b):
    return jnp.dot(a, b, preferred_element_type=F32)


def _dot_nt(a, b):
    return lax.dot_general(a, b, (((1,), (1,)), ((), ())), preferred_element_type=F32)


def _dot_tn(a, b):
    return lax.dot_general(a, b, (((0,), (0,)), ((), ())), preferred_element_type=F32)


def _iota2(shape, dim):
    return lax.broadcasted_iota(jnp.int32, shape, dim)


IN_DQKV = 0
IN_Q = IN_DQKV + 3 * DN_W
IN_KV = IN_Q + ATT_Q
IN_DA = IN_KV + 2 * ATT_KV
IN_DB = IN_DA + LANES
IN_DZ = IN_DB + LANES
IN_GATE = IN_DZ + DN_W
IN_MAIN = IN_GATE + 2 * D_MODEL + LANES
IN_PIECE = 4 * LANES


def _mask_pads(r, keep):
    return jnp.concatenate([r[:PAD_FRONT] * keep, r[PAD_FRONT:]], axis=0)


def _inproj_kernel(x_ref, nw_ref, w_ref, b_ref, convw_ref, alog_ref, dtb_ref,
                   q_ref, kv_ref, dq_ref, dk_ref, dv_ref, dz_ref, gate_ref, pack_ref, ext_ref):
    tile = TILE
    j = pl.program_id(1)
    keep = jnp.where(j == 0, 0.0, 1.0).astype(F32)

    @pl.when(j == 0)
    def _():
        ext_ref[0:SUBLANES, :] = jnp.zeros((SUBLANES, 3 * DN_W), F32)

    hn = _rms(x_ref[0], nw_ref[...]).astype(BF16)

    def proj(c0):
        cols = slice(c0, c0 + IN_PIECE)
        return _mask_pads(_dot(hn, w_ref[:, cols]) + b_ref[:, cols], keep)

    def after(v):
        bits = pltpu.bitcast(v[:SUBLANES, :LANES], jnp.uint32)
        return pltpu.bitcast(lax.shift_right_logical(bits, jnp.uint32(32)), F32)[:1]

    def conv_group(i, anchor):
        cols = slice(i * DN_HEAD_DIM, (i + 1) * DN_HEAD_DIM)
        x = ext_ref[:, cols]
        taps = [convw_ref[t:t + 1, cols] + anchor for t in range(CONV_WIDTH)]
        acc = x[SUBLANES:] * taps[CONV_WIDTH - 1]
        for shift in range(1, CONV_WIDTH):
            acc = acc + pltpu.roll(x, shift, 0)[SUBLANES:] * taps[CONV_WIDTH - 1 - shift]
        y = _silu(acc)
        kind, head = divmod(i, DN_HEADS)
        lanes = slice(head * DN_HEAD_DIM, (head + 1) * DN_HEAD_DIM)
        if kind == 2:
            dv_ref[0, :, lanes] = y.astype(BF16)
        else:
            scale = lax.rsqrt(jnp.sum(y * y, axis=-1, keepdims=True) + RMS_EPS)
            if kind == 0:
                dq_ref[0, :, lanes] = (y * (scale * (DN_HEAD_DIM ** -0.5))).astype(BF16)
            else:
                dk_ref[0, :, lanes] = (y * scale).astype(BF16)

    n_conv = (1, 1, 2, 1, 1, 2, 1, 1, 2)
    piece_starts = ([IN_DQKV + c0 for c0 in range(0, 3 * DN_W, IN_PIECE)] + [IN_Q, IN_KV, IN_DZ]
                    + [IN_GATE + c0 for c0 in range(0, 2 * D_MODEL, IN_PIECE)])
    group = 0
    prev = None
    kv_logits = None
    for n, c0 in enumerate(piece_starts):
        r = proj(c0)
        if c0 < IN_Q:
            ext_ref[SUBLANES:, c0 - IN_DQKV:c0 - IN_DQKV + IN_PIECE] = r
        elif c0 == IN_Q:
            q_ref[0] = (r * (ATT_HEAD_DIM ** -0.5)).astype(BF16)
        elif c0 == IN_KV:
            kv_logits = r
            kv_ref[0] = r[:, :2 * ATT_KV].astype(BF16)
        elif c0 == IN_DZ:
            dz_ref[0] = r.astype(BF16)
        else:
            gate_ref[0, :, c0 - IN_GATE:c0 - IN_GATE + IN_PIECE] = r.astype(BF16)
        if n >= 1:
            for _ in range(n_conv[n - 1]):
                conv_group(group, after(prev))
                group += 1
        prev = r
    ext_ref[0:SUBLANES, :] = ext_ref[tile:tile + SUBLANES, :]

    da = kv_logits[:, IN_DA - IN_KV:IN_DB - IN_KV]
    db = kv_logits[:, IN_DB - IN_KV:IN_DZ - IN_KV]
    g = -jnp.exp(alog_ref[...]) * jax.nn.softplus(da + dtb_ref[...])
    beta = jax.nn.sigmoid(db)
    ri = _iota2((tile, tile), 0)
    ci = _iota2((tile, tile), 1)
    same_chunk = (ri // DN_CHUNK) == (ci // DN_CHUNK)
    sums = jnp.concatenate([(same_chunk & (ri >= ci)).astype(BF16), same_chunk.astype(BF16)], axis=0)
    g_hi = g.astype(BF16)
    g_lo = (g - g_hi.astype(F32)).astype(BF16)
    both = _dot(sums, g_hi) + _dot(sums, g_lo)
    gcum = both[:tile]
    gl = both[tile:]
    grp = _iota2((tile, LANES), 1) // DN_HEADS
    pack = jnp.where(grp == PACK_GCUM, gcum,
                     jnp.where(grp == PACK_BETA, beta,
                               jnp.where(grp == PACK_EG, jnp.exp(gcum),
                                         jnp.where(grp == PACK_EDEC, jnp.exp(gl - gcum), jnp.exp(gl)))))
    pack_ref[0] = pack


def _inproj(h, nw, w, b, conv_w, alog, dtb, layer):
    B, L, D = h.shape
    widths = (ATT_Q, 2 * ATT_KV, DN_W, DN_W, DN_W, DN_W, 2 * D_MODEL, LANES)
    dtypes = (BF16,) * 7 + (F32,)
    return pl.pallas_call(
        _inproj_kernel,
        grid=(B, L // TILE),
        in_specs=[_rows(D), _resident(nw.shape, layer), _resident(w.shape, layer), _resident(b.shape, layer),
                  _resident(conv_w.shape, layer), _resident(alog.shape, layer), _resident(dtb.shape, layer)],
        out_specs=[_rows(width) for width in widths],
        out_shape=[jax.ShapeDtypeStruct((B, L, width), dt) for width, dt in zip(widths, dtypes)],
        scratch_shapes=[pltpu.VMEM((TILE + SUBLANES, 3 * DN_W), F32)],
        compiler_params=pltpu.CompilerParams(
            dimension_semantics=("parallel", "arbitrary"), vmem_limit_bytes=VMEM_LIMIT),
        name="inproj",
    )(h, nw, w, b, conv_w, alog, dtb)


ATT_ROWS = ATT_GROUP * BLOCK


def _attn_blocks(blocks, kv_meta, fills):
    qi = _iota2((ATT_ROWS, BLOCK), 0) % BLOCK
    kj = _iota2((ATT_ROWS, BLOCK), 1)
    upper = kj > qi
    meta_lanes = kj >= PAD_FRONT
    neg_inf = jnp.float32(-jnp.inf)

    chains = [(blk, g) for blk in blocks for g in range(ATT_KV_HEADS)]
    ksl = lambda g: slice(g * ATT_HEAD_DIM, (g + 1) * ATT_HEAD_DIM)
    vsl = lambda g: slice(ATT_KV + g * ATT_HEAD_DIM, ATT_KV + (g + 1) * ATT_HEAD_DIM)
    ones = jnp.ones((3 * BLOCK, ATT_HEAD_DIM), BF16)

    s_all = []
    for (q, kv_prev, kv_cur, _), g in chains:
        qg = jnp.concatenate([q[:, h * ATT_HEAD_DIM:(h + 1) * ATT_HEAD_DIM]
                              for h in range(g * ATT_GROUP, (g + 1) * ATT_GROUP)], axis=0)
        keys = jnp.concatenate([kv_prev[:, ksl(g)], kv_cur[:, ksl(g)], kv_meta[:, ksl(g)]], axis=0)
        s_all.append(_dot_nt(qg, keys))
    v_ext = [jnp.concatenate([jnp.concatenate([kv_prev[:, vsl(g)], kv_cur[:, vsl(g)], kv_meta[:, vsl(g)]],
                                              axis=0), ones], axis=1)
             for (_, kv_prev, kv_cur, _), g in chains]

    meta_ok = {}
    s_band, s_meta = [], []
    for (blk, g), s in zip(chains, s_all):
        n = blk[3]
        if id(blk) not in meta_ok:
            meta_ok[id(blk)] = meta_lanes & (kj <= qi + jnp.where(n == 0, 0, BLOCK))
        s_prev = s[:, :BLOCK] + jnp.where(n >= 2, 0.0, neg_inf)
        s_cur = s[:, BLOCK:2 * BLOCK] + jnp.where(n >= 1, 0.0, neg_inf)
        s_band.append(jnp.where(upper, s_prev, s_cur))
        s_meta.append(jnp.where(meta_ok[id(blk)], s[:, 2 * BLOCK:], fills[g]))
    m = [jnp.max(jnp.maximum(sb, sm), axis=-1, keepdims=True) for sb, sm in zip(s_band, s_meta)]
    p_all = []
    for sb, sm, mx in zip(s_band, s_meta, m):
        p_band = jnp.exp(sb - mx)
        p_all.append(jnp.concatenate([jnp.where(upper, p_band, 0.0).astype(BF16),
                                      jnp.where(upper, 0.0, p_band).astype(BF16),
                                      jnp.exp(sm - mx).astype(BF16)], axis=1))
    o_ext = [_dot(p, v) for p, v in zip(p_all, v_ext)]
    o = [(x * pltpu.roll(1.0 / x, ATT_HEAD_DIM, 1))[:, :ATT_HEAD_DIM] for x in o_ext]

    outs = []
    for b in range(len(blocks)):
        heads = [o[b * ATT_KV_HEADS + g][i * BLOCK:(i + 1) * BLOCK]
                 for g in range(ATT_KV_HEADS) for i in range(ATT_GROUP)]
        outs.append(jnp.concatenate(heads, axis=1).astype(BF16))
    return outs


def _attention_tile(sink_ref, q_ref, kv_ref, halo_ref, meta_ref, j):
    nblk = TILE // BLOCK
    lane = _iota2((BLOCK, BLOCK), 1)
    fills = [jnp.concatenate([jnp.where(lane == 0, sink_ref[g * ATT_GROUP + i], -jnp.inf)
                              for i in range(ATT_GROUP)], axis=0) for g in range(ATT_KV_HEADS)]
    rows = lambda i: slice(i * BLOCK, (i + 1) * BLOCK)
    blocks = [(q_ref[0, rows(i), :], halo_ref[0] if i == 0 else kv_ref[0, rows(i - 1), :],
               kv_ref[0, rows(i), :], j * nblk + i) for i in range(nblk)]
    return jnp.concatenate(_attn_blocks(blocks, meta_ref[0], fills), axis=0)


def _inverse_masks():
    C = DN_CHUNK
    ri = _iota2((C, C), 0)
    ci = _iota2((C, C), 1)
    same = lambda size: (ri // size) == (ci // size)
    levels = []
    size = DN_BASE
    while size < C:
        levels.append(same(2 * size) & ~same(size))
        size *= 2
    return (ri == ci).astype(F32), same(DN_BASE), levels


def _unit_lower_inverses(mats, masks):
    C = DN_CHUNK
    eye, base, levels = masks
    ds = [jnp.where(base, a, 0.0) for a in mats]
    ts = [eye - d for d in ds]
    d16 = [d.astype(BF16) for d in ds]
    powers = [_dot(d, d) for d in d16]
    size = 2
    while size < DN_BASE:
        pbs = [p.astype(BF16) for p in powers]
        if 2 * size < DN_BASE:
            both = [_dot(jnp.concatenate([t.astype(BF16), pb], axis=0), pb) for t, pb in zip(ts, pbs)]
            ts = [t + b[:C] for t, b in zip(ts, both)]
            powers = [b[C:] for b in both]
        else:
            ts = [t + _dot(t.astype(BF16), pb) for t, pb in zip(ts, pbs)]
        size *= 2
    for level in levels:
        es = [jnp.where(level, a, 0.0).astype(BF16) for a in mats]
        tbs = [t.astype(BF16) for t in ts]
        tes = [_dot(tb, e).astype(BF16) for tb, e in zip(tbs, es)]
        ts = [t - _dot(te, tb) for t, te, tb in zip(ts, tes, tbs)]
    return ts


def _deltanet_tile(q_ref, k_ref, v_ref, z_ref, pack_ref, nw_ref, state_ref):
    C = DN_CHUNK
    nchunk = TILE // C

    ri = _iota2((C, C), 0)
    ci = _iota2((C, C), 1)
    tril = ri >= ci
    strict = ri > ci
    masks = _inverse_masks()

    chains = [(c, h) for c in range(nchunk) for h in range(DN_HEADS)]
    rows = lambda c: slice(c * C, (c + 1) * C)
    lanes = lambda h: slice(h * DN_HEAD_DIM, (h + 1) * DN_HEAD_DIM)
    packs = [pack_ref[0, rows(c), :] for c in range(nchunk)]
    pack_ts = [p.T for p in packs]
    col = lambda c, h, grp: packs[c][:, DN_HEADS * grp + h:DN_HEADS * grp + h + 1]
    row = lambda c, h, grp: pack_ts[c][DN_HEADS * grp + h:DN_HEADS * grp + h + 1, :]

    q16 = [q_ref[0, rows(c), lanes(h)] for c, h in chains]
    k16 = [k_ref[0, rows(c), lanes(h)] for c, h in chains]
    v16 = [v_ref[0, rows(c), lanes(h)] for c, h in chains]
    decay = [jnp.exp(jnp.where(tril, col(c, h, PACK_GCUM) - row(c, h, PACK_GCUM), -jnp.inf)) for c, h in chains]
    kk = [_dot_nt(k, k) for k in k16]
    qk = [_dot_nt(q, k) for q, k in zip(q16, k16)]
    mats = [jnp.where(strict, x * d, 0.0) * col(c, h, PACK_BETA) for x, d, (c, h) in zip(kk, decay, chains)]
    attn = [(x * d).astype(BF16) for x, d in zip(qk, decay)]
    ts = _unit_lower_inverses(mats, masks)
    tb = [t * row(c, h, PACK_BETA) for t, (c, h) in zip(ts, chains)]
    us = [_dot(t.astype(BF16), v) for t, v in zip(tb, v16)]
    tw = [(t * row(c, h, PACK_EG)).astype(BF16) for t, (c, h) in zip(tb, chains)]
    ws = [_dot(t, k).astype(BF16) for t, k in zip(tw, k16)]
    q_dec = [(q.astype(F32) * col(c, h, PACK_EG)).astype(BF16) for q, (c, h) in zip(q16, chains)]
    k_dec = [(k.astype(F32) * col(c, h, PACK_EDEC)).astype(BF16) for k, (c, h) in zip(k16, chains)]

    ys = []
    for c in range(nchunk):
        idx = [c * DN_HEADS + h for h in range(DN_HEADS)]
        states = [state_ref[h] for h in range(DN_HEADS)]
        s16 = [s.astype(BF16) for s in states]
        both = [_dot(jnp.concatenate([ws[i], q_dec[i]], axis=0), s) for i, s in zip(idx, s16)]
        v_new = [(us[i] - b[:C]).astype(BF16) for i, b in zip(idx, both)]
        outs = [b[C:] + _dot(attn[i], vn) for i, b, vn in zip(idx, both, v_new)]
        for h, (i, s, vn) in enumerate(zip(idx, states, v_new)):
            egl = packs[c][C - 1:C, DN_HEADS * PACK_EGL + h:DN_HEADS * PACK_EGL + h + 1]
            state_ref[h] = s * egl + _dot_tn(k_dec[i], vn)
        ys.append(jnp.concatenate([(_rms(o, nw_ref[...]) * _silu(z_ref[0, rows(c), lanes(h)].astype(F32))
                                    ).astype(BF16) for h, o in enumerate(outs)], axis=1))
    return jnp.concatenate(ys, axis=0)


def _tail_kernel(sink_ref, q_ref, kv_ref, halo_ref, meta_ref, dq_ref, dk_ref, dv_ref, dz_ref, pack_ref,
                 dnw_ref, h_ref, gate_ref, wdown_ref, nmix_ref, npre_ref, wfi_ref, npost_ref,
                 out_ref, state_ref, yatt_ref, ydn_ref, *, n_tiles, tiles_per_seq):
    s = pl.program_id(0)
    j_mix = jnp.minimum(s, n_tiles - 1) % tiles_per_seq
    j_ffn = jnp.maximum(s - 1, 0) % tiles_per_seq

    @pl.when(s == 0)
    def _():
        yatt_ref[...] = jnp.zeros_like(yatt_ref)
        ydn_ref[...] = jnp.zeros_like(ydn_ref)

    @pl.when(j_mix == 0)
    def _():
        state_ref[...] = jnp.zeros_like(state_ref)

    keep = jnp.where(j_ffn == 0, 0.0, 1.0).astype(F32)
    d_ff = wfi_ref.shape[1] // 2
    r_att, r_dn, r_out = 0, ATT_Q, ATT_Q + DN_W
    r_ffn = r_out + D_MODEL
    wdown = lambda r0, r1: wdown_ref[r0:r1, :D_MODEL]
    ga = gate_ref[0, :, :D_MODEL].astype(F32)
    gb = gate_ref[0, :, D_MODEL:].astype(F32)
    merged = (jax.nn.sigmoid(ga) * _dot(yatt_ref[...], wdown(r_att, r_dn))
              + jax.nn.sigmoid(gb) * _dot(ydn_ref[...], wdown(r_dn, r_out)))
    mixed = _dot(merged.astype(BF16), wdown(r_out, r_ffn))
    x = h_ref[0] + _mask_pads(_rms(mixed, nmix_ref[...]), keep)
    hf = _rms(x, npre_ref[...]).astype(BF16)
    gate = _dot(hf, wfi_ref[:, :d_ff])
    up = _dot(hf, wfi_ref[:, d_ff:])
    ffn = _dot((_silu(gate) * up).astype(BF16), wdown(r_ffn, r_ffn + d_ff))
    out_ref[0] = x + _mask_pads(_rms(ffn, npost_ref[...]), keep)

    y_att = _attention_tile(sink_ref, q_ref, kv_ref, halo_ref, meta_ref, j_mix)
    y_dn = _deltanet_tile(dq_ref, dk_ref, dv_ref, dz_ref, pack_ref, dnw_ref, state_ref)
    yatt_ref[...] = y_att
    ydn_ref[...] = y_dn


def _layer_tail(h, q, kv, dq, dk, dv, dz, pack, gates, sinks, dn_norm_w, wdown, nmix, npre, wfi, npost, layer):
    B, L, D = h.shape
    tiles_per_seq = L // TILE
    n_tiles = B * tiles_per_seq
    nblk = TILE // BLOCK

    def mix_tile(s):
        t = jnp.minimum(s, n_tiles - 1)
        return t // tiles_per_seq, t % tiles_per_seq

    def ffn_tile(s):
        t = jnp.maximum(s - 1, 0)
        return t // tiles_per_seq, t % tiles_per_seq

    mix_rows = lambda width: pl.BlockSpec((1, TILE, width), lambda s: (*mix_tile(s), 0))
    ffn_rows = lambda width: pl.BlockSpec((1, TILE, width), lambda s: (*ffn_tile(s), 0))
    halo = pl.BlockSpec((1, BLOCK, 2 * ATT_KV),
                        lambda s: (mix_tile(s)[0], jnp.maximum(mix_tile(s)[1] * nblk - 1, 0), 0))
    meta = pl.BlockSpec((1, BLOCK, 2 * ATT_KV), lambda s: (mix_tile(s)[0], 0, 0))
    kernel_fn = lambda *refs: _tail_kernel(*refs, n_tiles=n_tiles, tiles_per_seq=tiles_per_seq)
    return pl.pallas_call(
        kernel_fn,
        grid=(n_tiles + 1,),
        in_specs=[pl.BlockSpec(memory_space=pltpu.SMEM),
                  mix_rows(ATT_Q), mix_rows(2 * ATT_KV), halo, meta,
                  mix_rows(DN_W), mix_rows(DN_W), mix_rows(DN_W), mix_rows(DN_W), mix_rows(LANES),
                  _resident(dn_norm_w.shape, layer),
                  ffn_rows(D), ffn_rows(2 * D),
                  _resident(wdown.shape, layer), _resident(nmix.shape, layer), _resident(npre.shape, layer),
                  _resident(wfi.shape, layer), _resident(npost.shape, layer)],
        out_specs=ffn_rows(D),
        out_shape=jax.ShapeDtypeStruct((B, L, D), F32),
        scratch_shapes=[pltpu.VMEM((DN_HEADS, DN_HEAD_DIM, DN_HEAD_DIM), F32),
                        pltpu.VMEM((TILE, ATT_Q), BF16), pltpu.VMEM((TILE, DN_W), BF16)],
        compiler_params=pltpu.CompilerParams(
            dimension_semantics=("arbitrary",), vmem_limit_bytes=VMEM_LIMIT),
        name="layer_tail",
    )(sinks, q, kv, kv, kv, dq, dk, dv, dz, pack, dn_norm_w, h, gates, wdown, nmix, npre, wfi, npost)


def _regroup_in_proj(t):
    d0 = ATT_Q + 2 * ATT_KV
    z0 = d0 + 3 * DN_W
    a0 = z0 + DN_W
    b0 = a0 + DN_HEADS
    g0 = b0 + DN_HEADS
    reps = (1,) * (t.ndim - 1) + (LANES // DN_HEADS,)
    return jnp.concatenate([t[..., d0:z0], t[..., :d0], jnp.tile(t[..., a0:b0], reps),
                            jnp.tile(t[..., b0:g0], reps), t[..., z0:a0], t[..., g0:],
                            jnp.zeros(t.shape[:-1] + (LANES,), t.dtype)], axis=-1)


def kernel(x, meta_tokens, w_in, b_in, conv_w, a_log, dt_bias, dn_norm_w, att_sinks, w_att_out, w_dn_out,
           w_out, norm_mix_pre, norm_mix_post, norm_ffn_pre, norm_ffn_post, w_ffn_in, w_ffn_out):
    B = x.shape[0]
    depth = w_in.shape[0]
    pads = jnp.zeros((B, PAD_FRONT, D_MODEL), x.dtype)
    meta = jnp.broadcast_to(meta_tokens.astype(x.dtype)[None], (B, N_META, D_MODEL))
    h = jnp.concatenate([pads, meta, x], axis=1)

    rows = lambda v: v.astype(F32)[:, None, :]
    lane_tiled = lambda v: jnp.tile(v.astype(F32), (1, LANES // DN_HEADS))[:, None, :]
    w_main = _regroup_in_proj(w_in).astype(BF16)
    b_main = rows(_regroup_in_proj(b_in))
    wdown = jnp.pad(jnp.concatenate([w_att_out, w_dn_out, w_out, w_ffn_out], axis=1).astype(BF16),
                    ((0, 0), (0, 0), (0, LANES)))
    wfi = w_ffn_in.astype(BF16)
    conv_w32, alog, dtb = conv_w.astype(F32), lane_tiled(a_log), lane_tiled(dt_bias)
    sinks = att_sinks.astype(F32)
    n_pre, n_dn, n_mix, n_ffn_pre, n_ffn_post = (rows(v) for v in (
        norm_mix_pre, dn_norm_w, norm_mix_post, norm_ffn_pre, norm_ffn_post))
    for l in range(depth):
        q, kv, dq, dk, dv, dz, gates, pack = _inproj(h, n_pre, w_main, b_main, conv_w32, alog, dtb, l)
        h = _layer_tail(h, q, kv, dq, dk, dv, dz, pack, gates, sinks[l], n_dn, wdown, n_mix, n_ffn_pre, wfi,
                        n_ffn_post, l)
    return h[:, PREFIX:]
```

```python
import functools

import jax
import jax.numpy as jnp
from jax import lax
from jax.experimental import pallas as pl
from jax.experimental.pallas import tpu as pltpu

D_MODEL = 1024
N_META = 16
BLOCK = 128
PREFIX = BLOCK
PAD_FRONT = PREFIX - N_META
ATT_HEADS = 8
ATT_KV_HEADS = 2
ATT_HEAD_DIM = 64
ATT_GROUP = ATT_HEADS // ATT_KV_HEADS
ATT_Q = ATT_HEADS * ATT_HEAD_DIM
ATT_KV = ATT_KV_HEADS * ATT_HEAD_DIM
DN_HEADS = 4
DN_HEAD_DIM = 128
DN_W = DN_HEADS * DN_HEAD_DIM
CONV_WIDTH = 4
RMS_EPS = 1e-6
DN_CHUNK = 128
DN_BASE = 16
LANES = 128
SUBLANES = 8
TILE = 512

PACK_GCUM, PACK_BETA, PACK_EG, PACK_EDEC, PACK_EGL = 0, 1, 2, 4, 6

F32 = jnp.float32
BF16 = jnp.bfloat16

VMEM_LIMIT = 56 * 1024 * 1024


def _resident(shape, layer=None):
    if layer is None:
        return pl.BlockSpec(shape, lambda *_: (0,) * len(shape), pipeline_mode=pl.Buffered(1))
    nd = len(shape) - 1
    return pl.BlockSpec((None,) + tuple(shape[1:]), lambda *_: (layer,) + (0,) * nd,
                        pipeline_mode=pl.Buffered(1))


def _rms(x, w):
    ms = jnp.mean(x * x, axis=-1, keepdims=True)
    return x * lax.rsqrt(ms + RMS_EPS) * w


def _silu(x):
    return x * jax.nn.sigmoid(x)


def _dot(a, b):
    return jnp.dot(a, b, preferred_element_type=F32)


def _dot_nt(a, b):
    return lax.dot_general(a, b, (((1,), (1,)), ((), ())), preferred_element_type=F32)


def _dot_tn(a, b):
    return lax.dot_general(a, b, (((0,), (0,)), ((), ())), preferred_element_type=F32)


def _iota2(shape, dim):
    return lax.broadcasted_iota(jnp.int32, shape, dim)


def _mask_pads(r, prefix):
    if not prefix:
        return r
    return jnp.concatenate([r[:PAD_FRONT] * 0.0, r[PAD_FRONT:]], axis=0)


IN_DQKV = 0
IN_Q = IN_DQKV + 3 * DN_W
IN_KV = IN_Q + ATT_Q
IN_LOGITS = IN_KV + 2 * ATT_KV
IN_DZ = IN_LOGITS + LANES
IN_GATE = IN_DZ + DN_W
IN_MAIN = IN_GATE + 2 * D_MODEL
IN_PIECE = 4 * LANES


def _inproj_kernel(x_ref, nw_ref, w_ref, b_ref, convw_ref, alog_ref, dtb_ref, convinit_ref,
                   q_ref, kv_ref, dq_ref, dk_ref, dv_ref, dz_ref, gate_ref, pack_ref, convtail_ref, ext_ref,
                   *, tile, prefix):
    j = pl.program_id(1)

    @pl.when(j == 0)
    def _():
        ext_ref[0:SUBLANES, :] = convinit_ref[0]

    hn = _rms(x_ref[0], nw_ref[...]).astype(BF16)

    def proj(c0):
        cols = slice(c0, IN_DZ if c0 == IN_KV else c0 + IN_PIECE)
        return _mask_pads(_dot(hn, w_ref[:, cols]) + b_ref[:, cols], prefix)

    def after(v):
        bits = pltpu.bitcast(v[:SUBLANES, :LANES], jnp.uint32)
        return pltpu.bitcast(lax.shift_right_logical(bits, jnp.uint32(32)), F32)[:1]

    def conv_group(i, anchor):
        cols = slice(i * DN_HEAD_DIM, (i + 1) * DN_HEAD_DIM)
        x = ext_ref[:, cols]
        taps = [convw_ref[t:t + 1, cols] + anchor for t in range(CONV_WIDTH)]
        acc = x[SUBLANES:] * taps[CONV_WIDTH - 1]
        for shift in range(1, CONV_WIDTH):
            acc = acc + pltpu.roll(x, shift, 0)[SUBLANES:] * taps[CONV_WIDTH - 1 - shift]
        y = _silu(acc)
        kind, head = divmod(i, DN_HEADS)
        lanes = slice(head * DN_HEAD_DIM, (head + 1) * DN_HEAD_DIM)
        if kind == 2:
            dv_ref[0, :, lanes] = y.astype(BF16)
        else:
            scale = lax.rsqrt(jnp.sum(y * y, axis=-1, keepdims=True) + RMS_EPS)
            if kind == 0:
                dq_ref[0, :, lanes] = (y * (scale * (DN_HEAD_DIM ** -0.5))).astype(BF16)
            else:
                dk_ref[0, :, lanes] = (y * scale).astype(BF16)

    n_conv = (1, 1, 2, 1, 1, 2, 1, 1, 2)
    piece_starts = ([IN_DQKV + c0 for c0 in range(0, 3 * DN_W, IN_PIECE)] + [IN_Q, IN_KV, IN_DZ]
                    + [IN_GATE + c0 for c0 in range(0, 2 * D_MODEL, IN_PIECE)])
    group = 0
    prev = None
    kv_logits = None
    for n, c0 in enumerate(piece_starts):
        r = proj(c0)
        if c0 < IN_Q:
            ext_ref[SUBLANES:, c0 - IN_DQKV:c0 - IN_DQKV + IN_PIECE] = r
        elif c0 == IN_Q:
            q_ref[0] = (r * (ATT_HEAD_DIM ** -0.5)).astype(BF16)
        elif c0 == IN_KV:
            kv_logits = r
            kv_ref[0] = r[:, :2 * ATT_KV].astype(BF16)
        elif c0 == IN_DZ:
            dz_ref[0] = r.astype(BF16)
        else:
            gate_ref[0, :, c0 - IN_GATE:c0 - IN_GATE + IN_PIECE] = r.astype(BF16)
        if n >= 1:
            for _ in range(n_conv[n - 1]):
                conv_group(group, after(prev))
                group += 1
        prev = r
    tail = ext_ref[tile:tile + SUBLANES, :]
    ext_ref[0:SUBLANES, :] = tail
    convtail_ref[0] = tail

    logits = kv_logits[:, IN_LOGITS - IN_KV:]
    g = -jnp.exp(alog_ref[...]) * jax.nn.softplus(logits + dtb_ref[...])
    beta = jax.nn.sigmoid(logits)
    ri = _iota2((tile, tile), 0)
    ci = _iota2((tile, tile), 1)
    same_chunk = (ri // DN_CHUNK) == (ci // DN_CHUNK)
    sums = jnp.concatenate([(same_chunk & (ri >= ci)).astype(BF16), same_chunk.astype(BF16)], axis=0)
    g_hi = g.astype(BF16)
    g_lo = (g - g_hi.astype(F32)).astype(BF16)
    both = _dot(sums, g_hi) + _dot(sums, g_lo)
    gcum = both[:tile]
    gl = both[tile:]
    grp = _iota2((tile, LANES), 1) // DN_HEADS
    pack = jnp.where(grp == PACK_GCUM, gcum,
                     jnp.where(grp == PACK_BETA, beta,
                               jnp.where(grp == PACK_EG, jnp.exp(gcum),
                                         jnp.where(grp == PACK_EDEC, jnp.exp(gl - gcum), jnp.exp(gl)))))
    pack_ref[0] = pack


def _inproj(h, nw, w, b, conv_w, alog, dtb, conv_init, layer, *, tile, prefix):
    B, L, D = h.shape
    rows = lambda width: pl.BlockSpec((1, tile, width), lambda b, j: (b, j, 0))
    widths = (ATT_Q, 2 * ATT_KV, DN_W, DN_W, DN_W, DN_W, 2 * D_MODEL, LANES)
    dtypes = (BF16,) * 7 + (F32,)
    conv_rows = lambda imap: pl.BlockSpec((1, SUBLANES, 3 * DN_W), imap)
    return pl.pallas_call(
        functools.partial(_inproj_kernel, tile=tile, prefix=prefix),
        grid=(B, L // tile),
        in_specs=[rows(D), _resident(nw.shape, layer), _resident(w.shape, layer), _resident(b.shape, layer),
                  _resident(conv_w.shape, layer), _resident(alog.shape, layer), _resident(dtb.shape, layer),
                  conv_rows(lambda b, j: (0, 0, 0))],
        out_specs=[rows(width) for width in widths] + [conv_rows(lambda b, j: (b, 0, 0))],
        out_shape=([jax.ShapeDtypeStruct((B, L, width), dt) for width, dt in zip(widths, dtypes)]
                   + [jax.ShapeDtypeStruct((B, SUBLANES, 3 * DN_W), F32)]),
        scratch_shapes=[pltpu.VMEM((tile + SUBLANES, 3 * DN_W), F32)],
        compiler_params=pltpu.CompilerParams(
            dimension_semantics=("parallel", "arbitrary"), vmem_limit_bytes=VMEM_LIMIT),
        name="inproj",
    )(h, nw, w, b, conv_w, alog, dtb, conv_init)


ATT_ROWS = ATT_GROUP * BLOCK


def _attn_blocks(blocks, kv_meta, fills, prefix):
    qi = _iota2((ATT_ROWS, BLOCK), 0) % BLOCK
    kj = _iota2((ATT_ROWS, BLOCK), 1)
    upper = kj > qi
    meta_ok = kj >= PAD_FRONT
    if prefix:
        meta_ok = meta_ok & (kj <= qi)
    neg_inf = jnp.float32(-jnp.inf)

    chains = [(blk, g) for blk in blocks for g in range(ATT_KV_HEADS)]
    ksl = lambda g: slice(g * ATT_HEAD_DIM, (g + 1) * ATT_HEAD_DIM)
    vsl = lambda g: slice(ATT_KV + g * ATT_HEAD_DIM, ATT_KV + (g + 1) * ATT_HEAD_DIM)
    ones = jnp.ones((3 * BLOCK, ATT_HEAD_DIM), BF16)

    s_all = []
    for (q, kv_prev, kv_cur, _), g in chains:
        qg = jnp.concatenate([q[:, h * ATT_HEAD_DIM:(h + 1) * ATT_HEAD_DIM]
                              for h in range(g * ATT_GROUP, (g + 1) * ATT_GROUP)], axis=0)
        keys = jnp.concatenate([kv_prev[:, ksl(g)], kv_cur[:, ksl(g)], kv_meta[:, ksl(g)]], axis=0)
        s_all.append(_dot_nt(qg, keys))
    v_ext = [jnp.concatenate([jnp.concatenate([kv_prev[:, vsl(g)], kv_cur[:, vsl(g)], kv_meta[:, vsl(g)]],
                                              axis=0), ones], axis=1)
             for (_, kv_prev, kv_cur, _), g in chains]

    s_band, s_meta = [], []
    for ((_, _, _, n), g), s in zip(chains, s_all):
        if prefix:
            s_band.append(jnp.full((ATT_ROWS, BLOCK), neg_inf))
        else:
            s_band.append(jnp.where(upper, s[:, :BLOCK] + jnp.where(n >= 1, 0.0, neg_inf), s[:, BLOCK:2 * BLOCK]))
        s_meta.append(jnp.where(meta_ok, s[:, 2 * BLOCK:], fills[g]))
    m = [jnp.max(jnp.maximum(sb, sm), axis=-1, keepdims=True) for sb, sm in zip(s_band, s_meta)]
    p_all = []
    for sb, sm, mx in zip(s_band, s_meta, m):
        p_band = jnp.exp(sb - mx)
        p_all.append(jnp.concatenate([jnp.where(upper, p_band, 0.0).astype(BF16),
                                      jnp.where(upper, 0.0, p_band).astype(BF16),
                                      jnp.exp(sm - mx).astype(BF16)], axis=1))
    o_ext = [_dot(p, v) for p, v in zip(p_all, v_ext)]
    o = [(x * pltpu.roll(1.0 / x, ATT_HEAD_DIM, 1))[:, :ATT_HEAD_DIM] for x in o_ext]

    outs = []
    for b in range(len(blocks)):
        heads = [o[b * ATT_KV_HEADS + g][i * BLOCK:(i + 1) * BLOCK]
                 for g in range(ATT_KV_HEADS) for i in range(ATT_GROUP)]
        outs.append(jnp.concatenate(heads, axis=1).astype(BF16))
    return outs


def _attention_tile(sink_ref, q_ref, kv_ref, halo_ref, meta_ref, j, tile, prefix):
    nblk = tile // BLOCK
    lane = _iota2((BLOCK, BLOCK), 1)
    fills = [jnp.concatenate([jnp.where(lane == 0, sink_ref[g * ATT_GROUP + i], -jnp.inf)
                              for i in range(ATT_GROUP)], axis=0) for g in range(ATT_KV_HEADS)]
    rows = lambda i: slice(i * BLOCK, (i + 1) * BLOCK)
    blocks = [(q_ref[0, rows(i), :], halo_ref[0] if i == 0 else kv_ref[0, rows(i - 1), :],
               kv_ref[0, rows(i), :], j * nblk + i) for i in range(nblk)]
    return jnp.concatenate(_attn_blocks(blocks, meta_ref[0], fills, prefix), axis=0)


def _inverse_masks():
    C = DN_CHUNK
    ri = _iota2((C, C), 0)
    ci = _iota2((C, C), 1)
    same = lambda size: (ri // size) == (ci // size)
    levels = []
    size = DN_BASE
    while size < C:
        levels.append(same(2 * size) & ~same(size))
        size *= 2
    return (ri == ci).astype(F32), same(DN_BASE), levels


def _unit_lower_inverses(mats, masks):
    C = DN_CHUNK
    eye, base, levels = masks
    ds = [jnp.where(base, a, 0.0) for a in mats]
    ts = [eye - d for d in ds]
    d16 = [d.astype(BF16) for d in ds]
    powers = [_dot(d, d) for d in d16]
    size = 2
    while size < DN_BASE:
        pbs = [p.astype(BF16) for p in powers]
        if 2 * size < DN_BASE:
            both = [_dot(jnp.concatenate([t.astype(BF16), pb], axis=0), pb) for t, pb in zip(ts, pbs)]
            ts = [t + b[:C] for t, b in zip(ts, both)]
            powers = [b[C:] for b in both]
        else:
            ts = [t + _dot(t.astype(BF16), pb) for t, pb in zip(ts, pbs)]
        size *= 2
    for level in levels:
        es = [jnp.where(level, a, 0.0).astype(BF16) for a in mats]
        tbs = [t.astype(BF16) for t in ts]
        tes = [_dot(tb, e).astype(BF16) for tb, e in zip(tbs, es)]
        ts = [t - _dot(te, tb) for t, te, tb in zip(ts, tes, tbs)]
    return ts


def _deltanet_tile(q_ref, k_ref, v_ref, z_ref, pack_ref, nw_ref, state_ref, tile):
    C = DN_CHUNK
    nchunk = tile // C

    ri = _iota2((C, C), 0)
    ci = _iota2((C, C), 1)
    tril = ri >= ci
    strict = ri > ci
    masks = _inverse_masks()

    chains = [(c, h) for c in range(nchunk) for h in range(DN_HEADS)]
    rows = lambda c: slice(c * C, (c + 1) * C)
    lanes = lambda h: slice(h * DN_HEAD_DIM, (h + 1) * DN_HEAD_DIM)
    packs = [pack_ref[0, rows(c), :] for c in range(nchunk)]
    pack_ts = [p.T for p in packs]
    col = lambda c, h, grp: packs[c][:, DN_HEADS * grp + h:DN_HEADS * grp + h + 1]
    row = lambda c, h, grp: pack_ts[c][DN_HEADS * grp + h:DN_HEADS * grp + h + 1, :]

    q16 = [q_ref[0, rows(c), lanes(h)] for c, h in chains]
    k16 = [k_ref[0, rows(c), lanes(h)] for c, h in chains]
    v16 = [v_ref[0, rows(c), lanes(h)] for c, h in chains]
    decay = [jnp.exp(jnp.where(tril, col(c, h, PACK_GCUM) - row(c, h, PACK_GCUM), -jnp.inf)) for c, h in chains]
    kk = [_dot_nt(k, k) for k in k16]
    qk = [_dot_nt(q, k) for q, k in zip(q16, k16)]
    mats = [jnp.where(strict, x * d, 0.0) * col(c, h, PACK_BETA) for x, d, (c, h) in zip(kk, decay, chains)]
    attn = [(x * d).astype(BF16) for x, d in zip(qk, decay)]
    ts = _unit_lower_inverses(mats, masks)
    tb = [(t * row(c, h, PACK_BETA)).astype(BF16) for t, (c, h) in zip(ts, chains)]
    k_eg = [(k.astype(F32) * col(c, h, PACK_EG)).astype(BF16) for k, (c, h) in zip(k16, chains)]
    uw = [_dot(t, jnp.concatenate([v, ke], axis=1)).astype(BF16) for t, v, ke in zip(tb, v16, k_eg)]
    q_dec = [q.astype(F32) * col(c, h, PACK_EG) for q, (c, h) in zip(q16, chains)]
    k_dec = [(k.astype(F32) * col(c, h, PACK_EDEC)).astype(BF16) for k, (c, h) in zip(k16, chains)]
    ktuw = [_dot_tn(kd, x) for kd, x in zip(k_dec, uw)]
    auw = [_dot(a, x) for a, x in zip(attn, uw)]
    lhs = [jnp.concatenate([-kt[:, DN_HEAD_DIM:], qd - a[:, DN_HEAD_DIM:]], axis=0).astype(BF16)
           for kt, qd, a in zip(ktuw, q_dec, auw)]

    ys = []
    for c in range(nchunk):
        idx = [c * DN_HEADS + h for h in range(DN_HEADS)]
        states = [state_ref[h] for h in range(DN_HEADS)]
        both = [_dot(lhs[i], s.astype(BF16)) for i, s in zip(idx, states)]
        for h, (i, s, b) in enumerate(zip(idx, states, both)):
            egl = packs[c][C - 1:C, DN_HEADS * PACK_EGL + h:DN_HEADS * PACK_EGL + h + 1]
            state_ref[h] = s * egl + b[:DN_HEAD_DIM] + ktuw[i][:, :DN_HEAD_DIM]
        outs = [b[DN_HEAD_DIM:] + auw[i][:, :DN_HEAD_DIM] for i, b in zip(idx, both)]
        ys.append(jnp.concatenate([(_rms(o, nw_ref[...]) * _silu(z_ref[0, rows(c), lanes(h)].astype(F32))
                                    ).astype(BF16) for h, o in enumerate(outs)], axis=1))
    return jnp.concatenate(ys, axis=0)


def _tail_kernel(sink_ref, q_ref, kv_ref, halo_ref, meta_ref, dq_ref, dk_ref, dv_ref, dz_ref, pack_ref,
                 dnw_ref, stateinit_ref, h_ref, gate_ref, wdown_ref, nmix_ref, npre_ref, wfi_ref, npost_ref,
                 out_ref, stateout_ref, state_ref, yatt_ref, ydn_ref,
                 *, tile, prefix, n_tiles, tiles_per_seq):
    s = pl.program_id(0)
    j_mix = jnp.minimum(s, n_tiles - 1) % tiles_per_seq

    @pl.when(s == 0)
    def _():
        yatt_ref[...] = jnp.zeros_like(yatt_ref)
        ydn_ref[...] = jnp.zeros_like(ydn_ref)

    @pl.when(j_mix == 0)
    def _():
        state_ref[...] = stateinit_ref[...]

    d_ff = wfi_ref.shape[1] // 2
    r_att, r_dn, r_out = 0, ATT_Q, ATT_Q + DN_W
    r_ffn = r_out + D_MODEL
    wdown = lambda r0, r1: wdown_ref[r0:r1, :D_MODEL]
    ga = gate_ref[0, :, :D_MODEL].astype(F32)
    gb = gate_ref[0, :, D_MODEL:].astype(F32)
    merged = (jax.nn.sigmoid(ga) * _dot(yatt_ref[...], wdown(r_att, r_dn))
              + jax.nn.sigmoid(gb) * _dot(ydn_ref[...], wdown(r_dn, r_out)))
    mixed = _dot(merged.astype(BF16), wdown(r_out, r_ffn))
    x = h_ref[0] + _mask_pads(_rms(mixed, nmix_ref[...]), prefix)
    hf = _rms(x, npre_ref[...]).astype(BF16)
    gate_up = _dot(hf, wfi_ref[...])
    ffn = _dot((_silu(gate_up[:, :d_ff]) * gate_up[:, d_ff:]).astype(BF16), wdown(r_ffn, r_ffn + d_ff))
    out_ref[0] = x + _mask_pads(_rms(ffn, npost_ref[...]), prefix)

    y_att = _attention_tile(sink_ref, q_ref, kv_ref, halo_ref, meta_ref, j_mix, tile, prefix)
    y_dn = _deltanet_tile(dq_ref, dk_ref, dv_ref, dz_ref, pack_ref, dnw_ref, state_ref, tile)
    yatt_ref[...] = y_att
    ydn_ref[...] = y_dn
    stateout_ref[...] = state_ref[...]


def _layer_tail(h, q, kv, kv_meta, dq, dk, dv, dz, pack, gates, sinks, dn_norm_w, state_init, wdown, nmix, npre,
                wfi, npost, layer, *, tile, prefix):
    B, L, D = h.shape
    tiles_per_seq = L // tile
    n_tiles = B * tiles_per_seq
    nblk = tile // BLOCK

    def mix_tile(s):
        t = jnp.minimum(s, n_tiles - 1)
        return t // tiles_per_seq, t % tiles_per_seq

    def ffn_tile(s):
        t = jnp.maximum(s - 1, 0)
        return t // tiles_per_seq, t % tiles_per_seq

    mix_rows = lambda width: pl.BlockSpec((1, tile, width), lambda s: (*mix_tile(s), 0))
    ffn_rows = lambda width: pl.BlockSpec((1, tile, width), lambda s: (*ffn_tile(s), 0))
    halo = pl.BlockSpec((1, BLOCK, 2 * ATT_KV),
                        lambda s: (mix_tile(s)[0], jnp.maximum(mix_tile(s)[1] * nblk - 1, 0), 0))
    state_shape = (DN_HEADS, DN_HEAD_DIM, DN_HEAD_DIM)
    kernel_fn = functools.partial(_tail_kernel, tile=tile, prefix=prefix, n_tiles=n_tiles,
                                  tiles_per_seq=tiles_per_seq)
    return pl.pallas_call(
        kernel_fn,
        grid=(n_tiles + 1,),
        in_specs=[pl.BlockSpec(memory_space=pltpu.SMEM),
                  mix_rows(ATT_Q), mix_rows(2 * ATT_KV), halo, _resident(kv_meta.shape),
                  mix_rows(DN_W), mix_rows(DN_W), mix_rows(DN_W), mix_rows(DN_W), mix_rows(LANES),
                  _resident(dn_norm_w.shape, layer), _resident(state_shape),
                  ffn_rows(D), ffn_rows(2 * D),
                  _resident(wdown.shape, layer), _resident(nmix.shape, layer), _resident(npre.shape, layer),
                  _resident(wfi.shape, layer), _resident(npost.shape, layer)],
        out_specs=[ffn_rows(D), pl.BlockSpec(state_shape, lambda s: (0, 0, 0))],
        out_shape=[jax.ShapeDtypeStruct((B, L, D), F32), jax.ShapeDtypeStruct(state_shape, F32)],
        scratch_shapes=[pltpu.VMEM(state_shape, F32),
                        pltpu.VMEM((tile, ATT_Q), BF16), pltpu.VMEM((tile, DN_W), BF16)],
        compiler_params=pltpu.CompilerParams(
            dimension_semantics=("arbitrary",), vmem_limit_bytes=VMEM_LIMIT),
        name="layer_tail",
    )(sinks, q, kv, kv, kv_meta, dq, dk, dv, dz, pack, dn_norm_w, state_init, h, gates, wdown, nmix, npre, wfi,
      npost)


def _regroup_in_proj(t):
    d0 = ATT_Q + 2 * ATT_KV
    z0 = d0 + 3 * DN_W
    a0 = z0 + DN_W
    g0 = a0 + 2 * DN_HEADS
    reps = (1,) * (t.ndim - 1) + (LANES // (2 * DN_HEADS),)
    return jnp.concatenate([t[..., d0:z0], t[..., :d0], jnp.tile(t[..., a0:g0], reps), t[..., z0:a0],
                            t[..., g0:]], axis=-1)


def kernel(x, meta_tokens, w_in, b_in, conv_w, a_log, dt_bias, dn_norm_w, att_sinks, w_att_out, w_dn_out,
           w_out, norm_mix_pre, norm_mix_post, norm_ffn_pre, norm_ffn_post, w_ffn_in, w_ffn_out):
    depth = w_in.shape[0]
    h_prefix = jnp.concatenate([jnp.zeros((PAD_FRONT, D_MODEL), x.dtype), meta_tokens.astype(x.dtype)])[None]
    h_main = x

    rows = lambda v: v.astype(F32)[:, None, :]
    lane_tiled = lambda v: jnp.tile(v.astype(F32), (1, LANES // DN_HEADS))[:, None, :]
    w_main = _regroup_in_proj(w_in).astype(BF16)
    b_main = rows(_regroup_in_proj(b_in))
    wdown = jnp.pad(jnp.concatenate([w_att_out, w_dn_out, w_out, w_ffn_out], axis=1).astype(BF16),
                    ((0, 0), (0, 0), (0, LANES)))
    wfi = w_ffn_in.astype(BF16)
    conv_w32, alog, dtb = conv_w.astype(F32), lane_tiled(a_log), lane_tiled(dt_bias)
    sinks = att_sinks.astype(F32)
    n_pre, n_dn, n_mix, n_ffn_pre, n_ffn_post = (rows(v) for v in (
        norm_mix_pre, dn_norm_w, norm_mix_post, norm_ffn_pre, norm_ffn_post))
    no_history = jnp.zeros((1, SUBLANES, 3 * DN_W), F32)
    no_state = jnp.zeros((DN_HEADS, DN_HEAD_DIM, DN_HEAD_DIM), F32)

    def layer(h, l, kv_meta, conv_init, state_init, tile, prefix):
        q, kv, dq, dk, dv, dz, gates, pack, conv_tail = _inproj(
            h, n_pre, w_main, b_main, conv_w32, alog, dtb, conv_init, l, tile=tile, prefix=prefix)
        h, state = _layer_tail(h, q, kv, kv if prefix else kv_meta, dq, dk, dv, dz, pack, gates, sinks[l], n_dn,
                               state_init, wdown, n_mix, n_ffn_pre, wfi, n_ffn_post, l, tile=tile, prefix=prefix)
        return h, kv, conv_tail, state

    for l in range(depth):
        h_prefix, kv_prefix, conv_prefix, state_prefix = layer(h_prefix, l, None, no_history, no_state,
                                                               PREFIX, True)
        h_main, _, _, _ = layer(h_main, l, kv_prefix, conv_prefix, state_prefix, TILE, False)
    return h_main
```

```python
import functools

import jax
import jax.numpy as jnp
from jax import lax
from jax.experimental import pallas as pl
from jax.experimental.pallas import tpu as pltpu

D_MODEL = 1024
N_META = 16
BLOCK = 128
PREFIX = BLOCK
PAD_FRONT = PREFIX - N_META
ATT_HEADS = 8
ATT_KV_HEADS = 2
ATT_HEAD_DIM = 64
ATT_GROUP = ATT_HEADS // ATT_KV_HEADS
ATT_Q = ATT_HEADS * ATT_HEAD_DIM
ATT_KV = ATT_KV_HEADS * ATT_HEAD_DIM
DN_HEADS = 4
DN_HEAD_DIM = 128
DN_W = DN_HEADS * DN_HEAD_DIM
CONV_WIDTH = 4
RMS_EPS = 1e-6
DN_CHUNK = 128
DN_BASE = 16
LANES = 128
SUBLANES = 8
TILE = 512

PACK_GCUM, PACK_BETA, PACK_EG, PACK_EDEC, PACK_EGL = 0, 1, 2, 4, 6

F32 = jnp.float32
BF16 = jnp.bfloat16

VMEM_LIMIT = 56 * 1024 * 1024


def _resident(shape, layer=None):
    if layer is None:
        return pl.BlockSpec(shape, lambda *_: (0,) * len(shape), pipeline_mode=pl.Buffered(1))
    nd = len(shape) - 1
    return pl.BlockSpec((None,) + tuple(shape[1:]), lambda *_: (layer,) + (0,) * nd,
                        pipeline_mode=pl.Buffered(1))


def _rms(x, w):
    ms = jnp.mean(x * x, axis=-1, keepdims=True)
    return x * lax.rsqrt(ms + RMS_EPS) * w


def _sigmoid(x):
    return 0.5 * jnp.tanh(0.5 * x) + 0.5


def _silu(x):
    half = 0.5 * x
    return half * jnp.tanh(half) + half


def _dot(a, b):
    return jnp.dot(a, b, preferred_element_type=F32)


def _dot_nt(a, b):
    return lax.dot_general(a, b, (((1,), (1,)), ((), ())), preferred_element_type=F32)


def _dot_tn(a, b):
    return lax.dot_general(a, b, (((0,), (0,)), ((), ())), preferred_element_type=F32)


def _iota2(shape, dim):
    return lax.broadcasted_iota(jnp.int32, shape, dim)


def _mask_pads(r, prefix):
    if not prefix:
        return r
    return jnp.concatenate([r[:PAD_FRONT] * 0.0, r[PAD_FRONT:]], axis=0)


IN_DQKV = 0
IN_Q = IN_DQKV + 3 * DN_W
IN_KV = IN_Q + ATT_Q
IN_LOGITS = IN_KV + 2 * ATT_KV
IN_DZ = IN_LOGITS + LANES
IN_GATE = IN_DZ + DN_W
IN_MAIN = IN_GATE + 2 * D_MODEL
IN_DOT_ROWS = 256


def _inproj_kernel(x_ref, nw_ref, w_ref, b_ref, convw_ref, alog_ref, dtb_ref, convinit_ref,
                   q_ref, kv_ref, dq_ref, dk_ref, dv_ref, dz_ref, gate_ref, pack_ref, convtail_ref, ext_ref,
                   *, tile, prefix):
    j = pl.program_id(1)

    @pl.when(j == 0)
    def _():
        ext_ref[0:SUBLANES, :] = convinit_ref[0]

    hn = _rms(x_ref[0], nw_ref[...]).astype(BF16)

    def proj(c0, c1):
        rows = min(tile, IN_DOT_ROWS)
        parts = [_dot(hn[r0:r0 + rows], w_ref[:, c0:c1]) + b_ref[:, c0:c1] for r0 in range(0, tile, rows)]
        return _mask_pads(jnp.concatenate(parts, axis=0), prefix)

    def after(v):
        bits = pltpu.bitcast(v[:SUBLANES, :LANES], jnp.uint32)
        return pltpu.bitcast(lax.shift_right_logical(bits, jnp.uint32(32)), F32)[:1]

    def conv_group(i, anchor):
        cols = slice(i * DN_HEAD_DIM, (i + 1) * DN_HEAD_DIM)
        x = ext_ref[:, cols]
        taps = [convw_ref[t:t + 1, cols] + anchor for t in range(CONV_WIDTH)]
        acc = x[SUBLANES:] * taps[CONV_WIDTH - 1]
        for shift in range(1, CONV_WIDTH):
            acc = acc + pltpu.roll(x, shift, 0)[SUBLANES:] * taps[CONV_WIDTH - 1 - shift]
        y = _silu(acc)
        kind, head = divmod(i, DN_HEADS)
        lanes = slice(head * DN_HEAD_DIM, (head + 1) * DN_HEAD_DIM)
        if kind == 2:
            dv_ref[0, :, lanes] = y.astype(BF16)
        else:
            scale = lax.rsqrt(jnp.sum(y * y, axis=-1, keepdims=True) + RMS_EPS)
            if kind == 0:
                dq_ref[0, :, lanes] = (y * (scale * (DN_HEAD_DIM ** -0.5))).astype(BF16)
            else:
                dk_ref[0, :, lanes] = (y * scale).astype(BF16)

    r_dqkv = proj(IN_DQKV, IN_Q)
    ext_ref[SUBLANES:, :] = r_dqkv
    r_mid = proj(IN_Q, IN_GATE)
    q_ref[0] = (r_mid[:, :ATT_Q] * (ATT_HEAD_DIM ** -0.5)).astype(BF16)
    kv_ref[0] = r_mid[:, IN_KV - IN_Q:IN_LOGITS - IN_Q].astype(BF16)
    logits = r_mid[:, IN_LOGITS - IN_Q:IN_DZ - IN_Q]
    dz_ref[0] = r_mid[:, IN_DZ - IN_Q:].astype(BF16)
    gate_ref[0] = proj(IN_GATE, IN_MAIN).astype(BF16)
    for i in range(3 * DN_HEADS):
        conv_group(i, after(r_mid))
    tail = ext_ref[tile:tile + SUBLANES, :]
    ext_ref[0:SUBLANES, :] = tail
    convtail_ref[0] = tail

    g = -jnp.exp(alog_ref[...]) * jax.nn.softplus(logits + dtb_ref[...])
    beta = _sigmoid(logits)
    ri = _iota2((tile, tile), 0)
    ci = _iota2((tile, tile), 1)
    same_chunk = (ri // DN_CHUNK) == (ci // DN_CHUNK)
    sums = jnp.concatenate([(same_chunk & (ri >= ci)).astype(BF16), same_chunk.astype(BF16)], axis=0)
    g_hi = g.astype(BF16)
    g_lo = (g - g_hi.astype(F32)).astype(BF16)
    both = _dot(sums, g_hi) + _dot(sums, g_lo)
    gcum = both[:tile]
    gl = both[tile:]
    grp = _iota2((tile, LANES), 1) // DN_HEADS
    pack = jnp.where(grp == PACK_GCUM, gcum,
                     jnp.where(grp == PACK_BETA, beta,
                               jnp.where(grp == PACK_EG, jnp.exp(gcum),
                                         jnp.where(grp == PACK_EDEC, jnp.exp(gl - gcum), jnp.exp(gl)))))
    pack_ref[0] = pack


def _inproj(h, nw, w, b, conv_w, alog, dtb, conv_init, layer, *, tile, prefix):
    B, L, D = h.shape
    rows = lambda width: pl.BlockSpec((1, tile, width), lambda b, j: (b, j, 0))
    widths = (ATT_Q, 2 * ATT_KV, DN_W, DN_W, DN_W, DN_W, 2 * D_MODEL, LANES)
    dtypes = (BF16,) * 7 + (F32,)
    conv_rows = lambda imap: pl.BlockSpec((1, SUBLANES, 3 * DN_W), imap)
    return pl.pallas_call(
        functools.partial(_inproj_kernel, tile=tile, prefix=prefix),
        grid=(B, L // tile),
        in_specs=[rows(D), _resident(nw.shape, layer), _resident(w.shape, layer), _resident(b.shape, layer),
                  _resident(conv_w.shape, layer), _resident(alog.shape, layer), _resident(dtb.shape, layer),
                  conv_rows(lambda b, j: (0, 0, 0))],
        out_specs=[rows(width) for width in widths] + [conv_rows(lambda b, j: (b, 0, 0))],
        out_shape=([jax.ShapeDtypeStruct((B, L, width), dt) for width, dt in zip(widths, dtypes)]
                   + [jax.ShapeDtypeStruct((B, SUBLANES, 3 * DN_W), F32)]),
        scratch_shapes=[pltpu.VMEM((tile + SUBLANES, 3 * DN_W), F32)],
        compiler_params=pltpu.CompilerParams(
            dimension_semantics=("parallel", "arbitrary"), vmem_limit_bytes=VMEM_LIMIT),
        name="inproj",
    )(h, nw, w, b, conv_w, alog, dtb, conv_init)


ATT_ROWS = ATT_GROUP * BLOCK


def _attn_blocks(blocks, kv_meta, fills, prefix):
    qi = _iota2((ATT_ROWS, BLOCK), 0) % BLOCK
    kj = _iota2((ATT_ROWS, BLOCK), 1)
    upper = kj > qi
    meta_ok = kj >= PAD_FRONT
    if prefix:
        meta_ok = meta_ok & (kj <= qi)
    neg_inf = jnp.float32(-jnp.inf)

    chains = [(blk, g) for blk in blocks for g in range(ATT_KV_HEADS)]
    ksl = lambda g: slice(g * ATT_HEAD_DIM, (g + 1) * ATT_HEAD_DIM)
    vsl = lambda g: slice(ATT_KV + g * ATT_HEAD_DIM, ATT_KV + (g + 1) * ATT_HEAD_DIM)
    ones = jnp.ones((3 * BLOCK, ATT_HEAD_DIM), BF16)

    s_all = []
    for (q, kv_prev, kv_cur, _), g in chains:
        qg = jnp.concatenate([q[:, h * ATT_HEAD_DIM:(h + 1) * ATT_HEAD_DIM]
                              for h in range(g * ATT_GROUP, (g + 1) * ATT_GROUP)], axis=0)
        keys = jnp.concatenate([kv_prev[:, ksl(g)], kv_cur[:, ksl(g)], kv_meta[:, ksl(g)]], axis=0)
        s_all.append(_dot_nt(qg, keys))
    v_ext = [jnp.concatenate([jnp.concatenate([kv_prev[:, vsl(g)], kv_cur[:, vsl(g)], kv_meta[:, vsl(g)]],
                                              axis=0), ones], axis=1)
             for (_, kv_prev, kv_cur, _), g in chains]

    s_band, s_meta = [], []
    for ((_, _, _, n), g), s in zip(chains, s_all):
        if prefix:
            s_band.append(jnp.full((ATT_ROWS, BLOCK), neg_inf))
        else:
            s_band.append(jnp.where(upper, s[:, :BLOCK] + jnp.where(n >= 1, 0.0, neg_inf), s[:, BLOCK:2 * BLOCK]))
        s_meta.append(jnp.where(meta_ok, s[:, 2 * BLOCK:], fills[g]))
    m = [jnp.max(jnp.maximum(sb, sm), axis=-1, keepdims=True) for sb, sm in zip(s_band, s_meta)]
    p_all = []
    for sb, sm, mx in zip(s_band, s_meta, m):
        p_band = jnp.exp(sb - mx)
        p_all.append(jnp.concatenate([jnp.where(upper, p_band, 0.0).astype(BF16),
                                      jnp.where(upper, 0.0, p_band).astype(BF16),
                                      jnp.exp(sm - mx).astype(BF16)], axis=1))
    o_ext = [_dot(p, v) for p, v in zip(p_all, v_ext)]
    o = [(x * pltpu.roll(1.0 / x, ATT_HEAD_DIM, 1))[:, :ATT_HEAD_DIM] for x in o_ext]

    outs = []
    for b in range(len(blocks)):
        heads = [o[b * ATT_KV_HEADS + g][i * BLOCK:(i + 1) * BLOCK]
                 for g in range(ATT_KV_HEADS) for i in range(ATT_GROUP)]
        outs.append(jnp.concatenate(heads, axis=1).astype(BF16))
    return outs


def _attention_tile(sink_ref, q_ref, kv_ref, halo_ref, meta_ref, j, tile, prefix):
    nblk = tile // BLOCK
    lane = _iota2((BLOCK, BLOCK), 1)
    fills = [jnp.concatenate([jnp.where(lane == 0, sink_ref[g * ATT_GROUP + i], -jnp.inf)
                              for i in range(ATT_GROUP)], axis=0) for g in range(ATT_KV_HEADS)]
    rows = lambda i: slice(i * BLOCK, (i + 1) * BLOCK)
    blocks = [(q_ref[0, rows(i), :], halo_ref[0] if i == 0 else kv_ref[0, rows(i - 1), :],
               kv_ref[0, rows(i), :], j * nblk + i) for i in range(nblk)]
    return jnp.concatenate(_attn_blocks(blocks, meta_ref[0], fills, prefix), axis=0)


def _inverse_masks():
    C = DN_CHUNK
    ri = _iota2((C, C), 0)
    ci = _iota2((C, C), 1)
    same = lambda size: (ri // size) == (ci // size)
    levels = []
    size = DN_BASE
    while size < C:
        levels.append(same(2 * size) & ~same(size))
        size *= 2
    return (ri == ci).astype(F32), same(DN_BASE), levels


def _unit_lower_inverses(mats, masks):
    C = DN_CHUNK
    eye, base, levels = masks
    ds = [jnp.where(base, a, 0.0) for a in mats]
    ts = [eye - d for d in ds]
    d16 = [d.astype(BF16) for d in ds]
    powers = [_dot(d, d) for d in d16]
    size = 2
    while size < DN_BASE:
        pbs = [p.astype(BF16) for p in powers]
        if 2 * size < DN_BASE:
            both = [_dot(jnp.concatenate([t.astype(BF16), pb], axis=0), pb) for t, pb in zip(ts, pbs)]
            ts = [t + b[:C] for t, b in zip(ts, both)]
            powers = [b[C:] for b in both]
        else:
            ts = [t + _dot(t.astype(BF16), pb) for t, pb in zip(ts, pbs)]
        size *= 2
    for level in levels:
        es = [jnp.where(level, a, 0.0).astype(BF16) for a in mats]
        tbs = [t.astype(BF16) for t in ts]
        tes = [_dot(tb, e).astype(BF16) for tb, e in zip(tbs, es)]
        ts = [t - _dot(te, tb) for t, te, tb in zip(ts, tes, tbs)]
    return ts


def _deltanet_tile(q_ref, k_ref, v_ref, z_ref, pack_ref, nw_ref, state_ref, tile):
    C = DN_CHUNK
    nchunk = tile // C

    ri = _iota2((C, C), 0)
    ci = _iota2((C, C), 1)
    tril = ri >= ci
    strict = ri > ci
    masks = _inverse_masks()

    chains = [(c, h) for c in range(nchunk) for h in range(DN_HEADS)]
    rows = lambda c: slice(c * C, (c + 1) * C)
    lanes = lambda h: slice(h * DN_HEAD_DIM, (h + 1) * DN_HEAD_DIM)
    packs = [pack_ref[0, rows(c), :] for c in range(nchunk)]
    pack_ts = [p.T for p in packs]
    col = lambda c, h, grp: packs[c][:, DN_HEADS * grp + h:DN_HEADS * grp + h + 1]
    row = lambda c, h, grp: pack_ts[c][DN_HEADS * grp + h:DN_HEADS * grp + h + 1, :]

    q16 = [q_ref[0, rows(c), lanes(h)] for c, h in chains]
    k16 = [k_ref[0, rows(c), lanes(h)] for c, h in chains]
    v16 = [v_ref[0, rows(c), lanes(h)] for c, h in chains]
    decay = [jnp.exp(jnp.where(tril, col(c, h, PACK_GCUM) - row(c, h, PACK_GCUM), -jnp.inf)) for c, h in chains]
    kk = [_dot_nt(k, k) for k in k16]
    qk = [_dot_nt(q, k) for q, k in zip(q16, k16)]
    mats = [jnp.where(strict, x * d, 0.0) * col(c, h, PACK_BETA) for x, d, (c, h) in zip(kk, decay, chains)]
    attn = [(x * d).astype(BF16) for x, d in zip(qk, decay)]
    ts = _unit_lower_inverses(mats, masks)
    tb = [(t * row(c, h, PACK_BETA)).astype(BF16) for t, (c, h) in zip(ts, chains)]
    k_eg = [(k.astype(F32) * col(c, h, PACK_EG)).astype(BF16) for k, (c, h) in zip(k16, chains)]
    uw = [_dot(t, jnp.concatenate([v, ke], axis=1)).astype(BF16) for t, v, ke in zip(tb, v16, k_eg)]
    q_dec = [q.astype(F32) * col(c, h, PACK_EG) for q, (c, h) in zip(q16, chains)]
    k_dec = [(k.astype(F32) * col(c, h, PACK_EDEC)).astype(BF16) for k, (c, h) in zip(k16, chains)]
    ktuw = [_dot_tn(kd, x) for kd, x in zip(k_dec, uw)]
    auw = [_dot(a, x) for a, x in zip(attn, uw)]
    lhs = [jnp.concatenate([-kt[:, DN_HEAD_DIM:], qd - a[:, DN_HEAD_DIM:]], axis=0).astype(BF16)
           for kt, qd, a in zip(ktuw, q_dec, auw)]

    ys = []
    for c in range(nchunk):
        idx = [c * DN_HEADS + h for h in range(DN_HEADS)]
        states = [state_ref[h] for h in range(DN_HEADS)]
        both = [_dot(lhs[i], s.astype(BF16)) for i, s in zip(idx, states)]
        for h, (i, s, b) in enumerate(zip(idx, states, both)):
            egl = packs[c][C - 1:C, DN_HEADS * PACK_EGL + h:DN_HEADS * PACK_EGL + h + 1]
            state_ref[h] = s * egl + b[:DN_HEAD_DIM] + ktuw[i][:, :DN_HEAD_DIM]
        outs = [b[DN_HEAD_DIM:] + auw[i][:, :DN_HEAD_DIM] for i, b in zip(idx, both)]
        ys.append(jnp.concatenate([(_rms(o, nw_ref[...]) * _silu(z_ref[0, rows(c), lanes(h)].astype(F32))
                                    ).astype(BF16) for h, o in enumerate(outs)], axis=1))
    return jnp.concatenate(ys, axis=0)


def _tail_kernel(sink_ref, q_ref, kv_ref, halo_ref, meta_ref, dq_ref, dk_ref, dv_ref, dz_ref, pack_ref,
                 dnw_ref, stateinit_ref, h_ref, gate_ref, wdown_ref, nmix_ref, npre_ref, wfi_ref, npost_ref,
                 out_ref, stateout_ref, state_ref, yatt_ref, ydn_ref,
                 *, tile, prefix, n_tiles, tiles_per_seq):
    s = pl.program_id(0)
    j_mix = jnp.minimum(s, n_tiles - 1) % tiles_per_seq

    @pl.when(s == 0)
    def _():
        yatt_ref[...] = jnp.zeros_like(yatt_ref)
        ydn_ref[...] = jnp.zeros_like(ydn_ref)

    @pl.when(j_mix == 0)
    def _():
        state_ref[...] = stateinit_ref[...]

    d_ff = wfi_ref.shape[1] // 2
    r_att, r_dn, r_out = 0, ATT_Q, ATT_Q + DN_W
    r_ffn = r_out + D_MODEL
    wdown = lambda r0, r1: wdown_ref[r0:r1, :D_MODEL]
    ga = gate_ref[0, :, :D_MODEL].astype(F32)
    gb = gate_ref[0, :, D_MODEL:].astype(F32)
    merged = (_sigmoid(ga) * _dot(yatt_ref[...], wdown(r_att, r_dn))
              + _sigmoid(gb) * _dot(ydn_ref[...], wdown(r_dn, r_out)))
    mixed = _dot(merged.astype(BF16), wdown(r_out, r_ffn))
    x = h_ref[0] + _mask_pads(_rms(mixed, nmix_ref[...]), prefix)
    hf = _rms(x, npre_ref[...]).astype(BF16)
    gate_up = _dot(hf, wfi_ref[...])
    ffn = _dot((_silu(gate_up[:, :d_ff]) * gate_up[:, d_ff:]).astype(BF16), wdown(r_ffn, r_ffn + d_ff))
    out_ref[0] = x + _mask_pads(_rms(ffn, npost_ref[...]), prefix)

    y_att = _attention_tile(sink_ref, q_ref, kv_ref, halo_ref, meta_ref, j_mix, tile, prefix)
    y_dn = _deltanet_tile(dq_ref, dk_ref, dv_ref, dz_ref, pack_ref, dnw_ref, state_ref, tile)
    yatt_ref[...] = y_att
    ydn_ref[...] = y_dn
    stateout_ref[...] = state_ref[...]


def _layer_tail(h, q, kv, kv_meta, dq, dk, dv, dz, pack, gates, sinks, dn_norm_w, state_init, wdown, nmix, npre,
                wfi, npost, layer, *, tile, prefix):
    B, L, D = h.shape
    tiles_per_seq = L // tile
    n_tiles = B * tiles_per_seq
    nblk = tile // BLOCK

    def mix_tile(s):
        t = jnp.minimum(s, n_tiles - 1)
        return t // tiles_per_seq, t % tiles_per_seq

    def ffn_tile(s):
        t = jnp.maximum(s - 1, 0)
        return t // tiles_per_seq, t % tiles_per_seq

    mix_rows = lambda width: pl.BlockSpec((1, tile, width), lambda s: (*mix_tile(s), 0))
    ffn_rows = lambda width: pl.BlockSpec((1, tile, width), lambda s: (*ffn_tile(s), 0))
    halo = pl.BlockSpec((1, BLOCK, 2 * ATT_KV),
                        lambda s: (mix_tile(s)[0], jnp.maximum(mix_tile(s)[1] * nblk - 1, 0), 0))
    state_shape = (DN_HEADS, DN_HEAD_DIM, DN_HEAD_DIM)
    kernel_fn = functools.partial(_tail_kernel, tile=tile, prefix=prefix, n_tiles=n_tiles,
                                  tiles_per_seq=tiles_per_seq)
    return pl.pallas_call(
        kernel_fn,
        grid=(n_tiles + 1,),
        in_specs=[pl.BlockSpec(memory_space=pltpu.SMEM),
                  mix_rows(ATT_Q), mix_rows(2 * ATT_KV), halo, _resident(kv_meta.shape),
                  mix_rows(DN_W), mix_rows(DN_W), mix_rows(DN_W), mix_rows(DN_W), mix_rows(LANES),
                  _resident(dn_norm_w.shape, layer), _resident(state_shape),
                  ffn_rows(D), ffn_rows(2 * D),
                  _resident(wdown.shape, layer), _resident(nmix.shape, layer), _resident(npre.shape, layer),
                  _resident(wfi.shape, layer), _resident(npost.shape, layer)],
        out_specs=[ffn_rows(D), pl.BlockSpec(state_shape, lambda s: (0, 0, 0))],
        out_shape=[jax.ShapeDtypeStruct((B, L, D), F32), jax.ShapeDtypeStruct(state_shape, F32)],
        scratch_shapes=[pltpu.VMEM(state_shape, F32),
                        pltpu.VMEM((tile, ATT_Q), BF16), pltpu.VMEM((tile, DN_W), BF16)],
        compiler_params=pltpu.CompilerParams(
            dimension_semantics=("arbitrary",), vmem_limit_bytes=VMEM_LIMIT),
        name="layer_tail",
    )(sinks, q, kv, kv, kv_meta, dq, dk, dv, dz, pack, dn_norm_w, state_init, h, gates, wdown, nmix, npre, wfi,
      npost)


def _regroup_in_proj(t):
    d0 = ATT_Q + 2 * ATT_KV
    z0 = d0 + 3 * DN_W
    a0 = z0 + DN_W
    g0 = a0 + 2 * DN_HEADS
    reps = (1,) * (t.ndim - 1) + (LANES // (2 * DN_HEADS),)
    return jnp.concatenate([t[..., d0:z0], t[..., :d0], jnp.tile(t[..., a0:g0], reps), t[..., z0:a0],
                            t[..., g0:]], axis=-1)


def kernel(x, meta_tokens, w_in, b_in, conv_w, a_log, dt_bias, dn_norm_w, att_sinks, w_att_out, w_dn_out,
           w_out, norm_mix_pre, norm_mix_post, norm_ffn_pre, norm_ffn_post, w_ffn_in, w_ffn_out):
    depth = w_in.shape[0]
    h_prefix = jnp.concatenate([jnp.zeros((PAD_FRONT, D_MODEL), x.dtype), meta_tokens.astype(x.dtype)])[None]
    h_main = x

    rows = lambda v: v.astype(F32)[:, None, :]
    lane_tiled = lambda v: jnp.tile(v.astype(F32), (1, LANES // DN_HEADS))[:, None, :]
    w_main = _regroup_in_proj(w_in).astype(BF16)
    b_main = rows(_regroup_in_proj(b_in))
    wdown = jnp.pad(jnp.concatenate([w_att_out, w_dn_out, w_out, w_ffn_out], axis=1).astype(BF16),
                    ((0, 0), (0, 0), (0, LANES)))
    wfi = w_ffn_in.astype(BF16)
    conv_w32, alog, dtb = conv_w.astype(F32), lane_tiled(a_log), lane_tiled(dt_bias)
    sinks = att_sinks.astype(F32)
    n_pre, n_dn, n_mix, n_ffn_pre, n_ffn_post = (rows(v) for v in (
        norm_mix_pre, dn_norm_w, norm_mix_post, norm_ffn_pre, norm_ffn_post))
    no_history = jnp.zeros((1, SUBLANES, 3 * DN_W), F32)
    no_state = jnp.zeros((DN_HEADS, DN_HEAD_DIM, DN_HEAD_DIM), F32)

    def layer(h, l, kv_meta, conv_init, state_init, tile, prefix):
        q, kv, dq, dk, dv, dz, gates, pack, conv_tail = _inproj(
            h, n_pre, w_main, b_main, conv_w32, alog, dtb, conv_init, l, tile=tile, prefix=prefix)
        h, state = _layer_tail(h, q, kv, kv if prefix else kv_meta, dq, dk, dv, dz, pack, gates, sinks[l], n_dn,
                               state_init, wdown, n_mix, n_ffn_pre, wfi, n_ffn_post, l, tile=tile, prefix=prefix)
        return h, kv, conv_tail, state

    for l in range(depth):
        h_prefix, kv_prefix, conv_prefix, state_prefix = layer(h_prefix, l, None, no_history, no_state,
                                                               PREFIX, True)
        h_main, _, _, _ = layer(h_main, l, kv_prefix, conv_prefix, state_prefix, TILE, False)
    return h_main
```

```python
import functools

import jax
import jax.numpy as jnp
from jax import lax
from jax.experimental import pallas as pl
from jax.experimental.pallas import tpu as pltpu

D_MODEL = 1024
N_META = 16
BLOCK = 128
PREFIX = BLOCK
PAD_FRONT = PREFIX - N_META
ATT_HEADS = 8
ATT_KV_HEADS = 2
ATT_HEAD_DIM = 64
ATT_GROUP = ATT_HEADS // ATT_KV_HEADS
ATT_Q = ATT_HEADS * ATT_HEAD_DIM
ATT_KV = ATT_KV_HEADS * ATT_HEAD_DIM
DN_HEADS = 4
DN_HEAD_DIM = 128
DN_W = DN_HEADS * DN_HEAD_DIM
CONV_WIDTH = 4
RMS_EPS = 1e-6
DN_CHUNK = 128
DN_BASE = 16
LANES = 128
SUBLANES = 8
TILE = 512

PACK_GCUM, PACK_BETA, PACK_EG, PACK_EDEC, PACK_EGL = 0, 1, 2, 4, 6

F32 = jnp.float32
BF16 = jnp.bfloat16

VMEM_LIMIT = 56 * 1024 * 1024


def _resident(shape, layer=None):
    if layer is None:
        return pl.BlockSpec(shape, lambda *_: (0,) * len(shape), pipeline_mode=pl.Buffered(1))
    nd = len(shape) - 1
    return pl.BlockSpec((None,) + tuple(shape[1:]), lambda *_: (layer,) + (0,) * nd,
                        pipeline_mode=pl.Buffered(1))


def _rms(x, w):
    ms = jnp.mean(x * x, axis=-1, keepdims=True)
    return x * lax.rsqrt(ms + RMS_EPS) * w


def _sigmoid(x):
    return 0.5 * jnp.tanh(0.5 * x) + 0.5


def _silu(x):
    half = 0.5 * x
    return half * jnp.tanh(half) + half


def _dot(a, b):
    return jnp.dot(a, b, preferred_element_type=F32)


def _dot_nt(a, b):
    return lax.dot_general(a, b, (((1,), (1,)), ((), ())), preferred_element_type=F32)


def _dot_tn(a, b):
    return lax.dot_general(a, b, (((0,), (0,)), ((), ())), preferred_element_type=F32)


def _iota2(shape, dim):
    return lax.broadcasted_iota(jnp.int32, shape, dim)


def _mask_pads(r, prefix):
    if not prefix:
        return r
    return jnp.concatenate([r[:PAD_FRONT] * 0.0, r[PAD_FRONT:]], axis=0)


IN_DQKV = 0
IN_Q = IN_DQKV + 3 * DN_W
IN_KV = IN_Q + ATT_Q
IN_LOGITS = IN_KV + 2 * ATT_KV
IN_DZ = IN_LOGITS + LANES
IN_GATE = IN_DZ + DN_W
IN_MAIN = IN_GATE + 2 * D_MODEL
DOT_ROWS = 256


def _inproj_kernel(x_ref, nw_ref, w_ref, b_ref, convw_ref, alog_ref, dtb_ref, convinit_ref,
                   q_ref, kv_ref, dq_ref, dk_ref, dv_ref, dz_ref, gate_ref, pack_ref, convtail_ref, ext_ref,
                   *, tile, prefix):
    j = pl.program_id(1)

    @pl.when(j == 0)
    def _():
        ext_ref[0:SUBLANES, :] = convinit_ref[0]

    hn = _rms(x_ref[0], nw_ref[...]).astype(BF16)

    def proj(c0, c1):
        rows = min(tile, DOT_ROWS)
        parts = [_dot(hn[r0:r0 + rows], w_ref[:, c0:c1]) + b_ref[:, c0:c1] for r0 in range(0, tile, rows)]
        return _mask_pads(jnp.concatenate(parts, axis=0), prefix)

    def after(v):
        bits = pltpu.bitcast(v[:SUBLANES, :LANES], jnp.uint32)
        return pltpu.bitcast(lax.shift_right_logical(bits, jnp.uint32(32)), F32)[:1]

    def conv_group(i, anchor):
        cols = slice(i * DN_HEAD_DIM, (i + 1) * DN_HEAD_DIM)
        x = ext_ref[:, cols]
        taps = [convw_ref[t:t + 1, cols] + anchor for t in range(CONV_WIDTH)]
        acc = x[SUBLANES:] * taps[CONV_WIDTH - 1]
        for shift in range(1, CONV_WIDTH):
            acc = acc + pltpu.roll(x, shift, 0)[SUBLANES:] * taps[CONV_WIDTH - 1 - shift]
        y = _silu(acc)
        kind, head = divmod(i, DN_HEADS)
        lanes = slice(head * DN_HEAD_DIM, (head + 1) * DN_HEAD_DIM)
        if kind == 2:
            dv_ref[0, :, lanes] = y.astype(BF16)
        else:
            scale = lax.rsqrt(jnp.sum(y * y, axis=-1, keepdims=True) + RMS_EPS)
            if kind == 0:
                dq_ref[0, :, lanes] = (y * (scale * (DN_HEAD_DIM ** -0.5))).astype(BF16)
            else:
                dk_ref[0, :, lanes] = (y * scale).astype(BF16)

    r_dqkv = proj(IN_DQKV, IN_Q)
    ext_ref[SUBLANES:, :] = r_dqkv
    r_mid = proj(IN_Q, IN_GATE)
    q_ref[0] = (r_mid[:, :ATT_Q] * (ATT_HEAD_DIM ** -0.5)).astype(BF16)
    kv_ref[0] = r_mid[:, IN_KV - IN_Q:IN_LOGITS - IN_Q].astype(BF16)
    logits = r_mid[:, IN_LOGITS - IN_Q:IN_DZ - IN_Q]
    dz_ref[0] = r_mid[:, IN_DZ - IN_Q:].astype(BF16)
    gate_ref[0] = proj(IN_GATE, IN_MAIN).astype(BF16)
    for i in range(3 * DN_HEADS):
        conv_group(i, after(r_mid))
    tail = ext_ref[tile:tile + SUBLANES, :]
    ext_ref[0:SUBLANES, :] = tail
    convtail_ref[0] = tail

    g = -jnp.exp(alog_ref[...]) * jax.nn.softplus(logits + dtb_ref[...])
    beta = _sigmoid(logits)
    ri = _iota2((tile, tile), 0)
    ci = _iota2((tile, tile), 1)
    same_chunk = (ri // DN_CHUNK) == (ci // DN_CHUNK)
    sums = jnp.concatenate([(same_chunk & (ri >= ci)).astype(BF16), same_chunk.astype(BF16)], axis=0)
    g_hi = g.astype(BF16)
    g_lo = (g - g_hi.astype(F32)).astype(BF16)
    both = _dot(sums, g_hi) + _dot(sums, g_lo)
    gcum = both[:tile]
    gl = both[tile:]
    grp = _iota2((tile, LANES), 1) // DN_HEADS
    pack = jnp.where(grp == PACK_GCUM, gcum,
                     jnp.where(grp == PACK_BETA, beta,
                               jnp.where(grp == PACK_EG, jnp.exp(gcum),
                                         jnp.where(grp == PACK_EDEC, jnp.exp(gl - gcum), jnp.exp(gl)))))
    pack_ref[0] = pack


def _inproj(h, nw, w, b, conv_w, alog, dtb, conv_init, layer, *, tile, prefix):
    B, L, D = h.shape
    rows = lambda width: pl.BlockSpec((1, tile, width), lambda b, j: (b, j, 0))
    widths = (ATT_Q, 2 * ATT_KV, DN_W, DN_W, DN_W, DN_W, 2 * D_MODEL, LANES)
    dtypes = (BF16,) * 7 + (F32,)
    conv_rows = lambda imap: pl.BlockSpec((1, SUBLANES, 3 * DN_W), imap)
    return pl.pallas_call(
        functools.partial(_inproj_kernel, tile=tile, prefix=prefix),
        grid=(B, L // tile),
        in_specs=[rows(D), _resident(nw.shape, layer), _resident(w.shape, layer), _resident(b.shape, layer),
                  _resident(conv_w.shape, layer), _resident(alog.shape, layer), _resident(dtb.shape, layer),
                  conv_rows(lambda b, j: (0, 0, 0))],
        out_specs=[rows(width) for width in widths] + [conv_rows(lambda b, j: (b, 0, 0))],
        out_shape=([jax.ShapeDtypeStruct((B, L, width), dt) for width, dt in zip(widths, dtypes)]
                   + [jax.ShapeDtypeStruct((B, SUBLANES, 3 * DN_W), F32)]),
        scratch_shapes=[pltpu.VMEM((tile + SUBLANES, 3 * DN_W), F32)],
        compiler_params=pltpu.CompilerParams(
            dimension_semantics=("parallel", "arbitrary"), vmem_limit_bytes=VMEM_LIMIT),
        name="inproj",
    )(h, nw, w, b, conv_w, alog, dtb, conv_init)


ATT_ROWS = ATT_GROUP * BLOCK


def _attn_blocks(blocks, kv_meta, fills, prefix):
    qi = _iota2((ATT_ROWS, BLOCK), 0) % BLOCK
    kj = _iota2((ATT_ROWS, BLOCK), 1)
    upper = kj > qi
    meta_ok = kj >= PAD_FRONT
    if prefix:
        meta_ok = meta_ok & (kj <= qi)
    neg_inf = jnp.float32(-jnp.inf)

    chains = [(blk, g) for blk in blocks for g in range(ATT_KV_HEADS)]
    ksl = lambda g: slice(g * ATT_HEAD_DIM, (g + 1) * ATT_HEAD_DIM)
    vsl = lambda g: slice(ATT_KV + g * ATT_HEAD_DIM, ATT_KV + (g + 1) * ATT_HEAD_DIM)
    zeros = jnp.zeros((BLOCK, ATT_HEAD_DIM), BF16)
    ones = jnp.ones((BLOCK, ATT_HEAD_DIM), BF16)

    s_all = []
    for (q, kv_prev, kv_cur, _), g in chains:
        qg = jnp.concatenate([q[:, h * ATT_HEAD_DIM:(h + 1) * ATT_HEAD_DIM]
                              for h in range(g * ATT_GROUP, (g + 1) * ATT_GROUP)], axis=0)
        keys = jnp.concatenate([kv_prev[:, ksl(g)], kv_cur[:, ksl(g)], kv_meta[:, ksl(g)]], axis=0)
        s_all.append(_dot_nt(qg, keys))
    v_ext = []
    for (_, kv_prev, kv_cur, _), g in chains:
        v_diff = (kv_prev[:, vsl(g)].astype(F32) - kv_cur[:, vsl(g)].astype(F32)).astype(BF16)
        v_ext.append(jnp.concatenate([jnp.concatenate([kv_cur[:, vsl(g)], ones], axis=1),
                                      jnp.concatenate([v_diff, zeros], axis=1),
                                      jnp.concatenate([kv_meta[:, vsl(g)], ones], axis=1)], axis=0))

    s_band, s_meta = [], []
    for ((_, _, _, n), g), s in zip(chains, s_all):
        if prefix:
            s_band.append(jnp.full((ATT_ROWS, BLOCK), neg_inf))
        else:
            s_band.append(jnp.where(upper, s[:, :BLOCK] + jnp.where(n >= 1, 0.0, neg_inf), s[:, BLOCK:2 * BLOCK]))
        s_meta.append(jnp.where(meta_ok, s[:, 2 * BLOCK:], fills[g]))
    m = [jnp.max(jnp.maximum(sb, sm), axis=-1, keepdims=True) for sb, sm in zip(s_band, s_meta)]
    p_all = []
    for sb, sm, mx in zip(s_band, s_meta, m):
        p_band = jnp.exp(sb - mx)
        p_all.append(jnp.concatenate([p_band.astype(BF16), jnp.where(upper, p_band, 0.0).astype(BF16),
                                      jnp.exp(sm - mx).astype(BF16)], axis=1))
    o_ext = [_dot(p, v) for p, v in zip(p_all, v_ext)]
    o = [(x * pltpu.roll(1.0 / x, ATT_HEAD_DIM, 1))[:, :ATT_HEAD_DIM] for x in o_ext]

    outs = []
    for b in range(len(blocks)):
        heads = [o[b * ATT_KV_HEADS + g][i * BLOCK:(i + 1) * BLOCK]
                 for g in range(ATT_KV_HEADS) for i in range(ATT_GROUP)]
        outs.append(jnp.concatenate(heads, axis=1).astype(BF16))
    return outs


def _attention_tile(sink_ref, q_ref, kv_ref, halo_ref, meta_ref, j, tile, prefix):
    nblk = tile // BLOCK
    lane = _iota2((BLOCK, BLOCK), 1)
    fills = [jnp.concatenate([jnp.where(lane == 0, sink_ref[g * ATT_GROUP + i], -jnp.inf)
                              for i in range(ATT_GROUP)], axis=0) for g in range(ATT_KV_HEADS)]
    rows = lambda i: slice(i * BLOCK, (i + 1) * BLOCK)
    blocks = [(q_ref[0, rows(i), :], halo_ref[0] if i == 0 else kv_ref[0, rows(i - 1), :],
               kv_ref[0, rows(i), :], j * nblk + i) for i in range(nblk)]
    return jnp.concatenate(_attn_blocks(blocks, meta_ref[0], fills, prefix), axis=0)


def _inverse_masks():
    C = DN_CHUNK
    ri = _iota2((C, C), 0)
    ci = _iota2((C, C), 1)
    same = lambda size: (ri // size) == (ci // size)
    levels = []
    size = DN_BASE
    while size < C:
        levels.append(same(2 * size) & ~same(size))
        size *= 2
    return (ri == ci).astype(F32), same(DN_BASE), levels


def _unit_lower_inverses(mats, masks):
    C = DN_CHUNK
    eye, base, levels = masks
    ds = [jnp.where(base, a, 0.0) for a in mats]
    ts = [eye - d for d in ds]
    d16 = [d.astype(BF16) for d in ds]
    powers = [_dot(d, d) for d in d16]
    size = 2
    while size < DN_BASE:
        pbs = [p.astype(BF16) for p in powers]
        if 2 * size < DN_BASE:
            both = [_dot(jnp.concatenate([t.astype(BF16), pb], axis=0), pb) for t, pb in zip(ts, pbs)]
            ts = [t + b[:C] for t, b in zip(ts, both)]
            powers = [b[C:] for b in both]
        else:
            ts = [t + _dot(t.astype(BF16), pb) for t, pb in zip(ts, pbs)]
        size *= 2
    for level in levels:
        es = [jnp.where(level, a, 0.0).astype(BF16) for a in mats]
        tbs = [t.astype(BF16) for t in ts]
        tes = [_dot(tb, e).astype(BF16) for tb, e in zip(tbs, es)]
        ts = [t - _dot(te, tb) for t, te, tb in zip(ts, tes, tbs)]
    return ts


def _deltanet_scores(q_ref, k_ref, tile):
    C = DN_CHUNK
    chains = [(c, h) for c in range(tile // C) for h in range(DN_HEADS)]
    q16 = [q_ref[0, c * C:(c + 1) * C, h * DN_HEAD_DIM:(h + 1) * DN_HEAD_DIM] for c, h in chains]
    k16 = [k_ref[0, c * C:(c + 1) * C, h * DN_HEAD_DIM:(h + 1) * DN_HEAD_DIM] for c, h in chains]
    return [_dot_nt(k, k) for k in k16], [_dot_nt(q, k) for q, k in zip(q16, k16)]


def _deltanet_tile(scores, q_ref, k_ref, v_ref, z_ref, pack_ref, nw_ref, state_ref, tile):
    C = DN_CHUNK
    nchunk = tile // C

    ri = _iota2((C, C), 0)
    ci = _iota2((C, C), 1)
    tril = ri >= ci
    strict = ri > ci
    masks = _inverse_masks()

    chains = [(c, h) for c in range(nchunk) for h in range(DN_HEADS)]
    rows = lambda c: slice(c * C, (c + 1) * C)
    lanes = lambda h: slice(h * DN_HEAD_DIM, (h + 1) * DN_HEAD_DIM)
    packs = [pack_ref[0, rows(c), :] for c in range(nchunk)]
    pack_ts = [p.T for p in packs]
    col = lambda c, h, grp: packs[c][:, DN_HEADS * grp + h:DN_HEADS * grp + h + 1]
    row = lambda c, h, grp: pack_ts[c][DN_HEADS * grp + h:DN_HEADS * grp + h + 1, :]

    q16 = [q_ref[0, rows(c), lanes(h)] for c, h in chains]
    k16 = [k_ref[0, rows(c), lanes(h)] for c, h in chains]
    v16 = [v_ref[0, rows(c), lanes(h)] for c, h in chains]
    decay = [jnp.exp(jnp.where(tril, col(c, h, PACK_GCUM) - row(c, h, PACK_GCUM), -jnp.inf)) for c, h in chains]
    kk, qk = scores
    mats = [jnp.where(strict, x * d, 0.0) * col(c, h, PACK_BETA) for x, d, (c, h) in zip(kk, decay, chains)]
    attn = [(x * d).astype(BF16) for x, d in zip(qk, decay)]
    ts = _unit_lower_inverses(mats, masks)
    tb = [(t * row(c, h, PACK_BETA)).astype(BF16) for t, (c, h) in zip(ts, chains)]
    k_eg = [(k.astype(F32) * col(c, h, PACK_EG)).astype(BF16) for k, (c, h) in zip(k16, chains)]
    uw = [_dot(t, jnp.concatenate([v, ke], axis=1)).astype(BF16) for t, v, ke in zip(tb, v16, k_eg)]
    q_dec = [q.astype(F32) * col(c, h, PACK_EG) for q, (c, h) in zip(q16, chains)]
    k_dec = [(k.astype(F32) * col(c, h, PACK_EDEC)).astype(BF16) for k, (c, h) in zip(k16, chains)]
    ktuw = [_dot_tn(kd, x) for kd, x in zip(k_dec, uw)]
    auw = [_dot(a, x) for a, x in zip(attn, uw)]
    lhs = [jnp.concatenate([-kt[:, DN_HEAD_DIM:], qd - a[:, DN_HEAD_DIM:]], axis=0).astype(BF16)
           for kt, qd, a in zip(ktuw, q_dec, auw)]

    ys = []
    for c in range(nchunk):
        idx = [c * DN_HEADS + h for h in range(DN_HEADS)]
        states = [state_ref[h] for h in range(DN_HEADS)]
        both = [_dot(lhs[i], s.astype(BF16)) for i, s in zip(idx, states)]
        for h, (i, s, b) in enumerate(zip(idx, states, both)):
            egl = packs[c][C - 1:C, DN_HEADS * PACK_EGL + h:DN_HEADS * PACK_EGL + h + 1]
            state_ref[h] = s * egl + b[:DN_HEAD_DIM] + ktuw[i][:, :DN_HEAD_DIM]
        outs = [b[DN_HEAD_DIM:] + auw[i][:, :DN_HEAD_DIM] for i, b in zip(idx, both)]
        ys.append(jnp.concatenate([(_rms(o, nw_ref[...]) * _silu(z_ref[0, rows(c), lanes(h)].astype(F32))
                                    ).astype(BF16) for h, o in enumerate(outs)], axis=1))
    return jnp.concatenate(ys, axis=0)


def _tail_kernel(sink_ref, q_ref, kv_ref, halo_ref, meta_ref, dq_ref, dk_ref, dv_ref, dz_ref, pack_ref,
                 dnw_ref, stateinit_ref, h_ref, gate_ref, wdown_ref, nmix_ref, npre_ref, wfi_ref, npost_ref,
                 out_ref, stateout_ref, state_ref, yatt_ref, ydn_ref,
                 *, tile, prefix, n_tiles, tiles_per_seq):
    s = pl.program_id(0)
    j_mix = jnp.minimum(s, n_tiles - 1) % tiles_per_seq

    @pl.when(s == 0)
    def _():
        yatt_ref[...] = jnp.zeros_like(yatt_ref)
        ydn_ref[...] = jnp.zeros_like(ydn_ref)

    @pl.when(j_mix == 0)
    def _():
        state_ref[...] = stateinit_ref[...]

    d_ff = wfi_ref.shape[1] // 2
    r_att, r_dn, r_out = 0, ATT_Q, ATT_Q + DN_W
    r_ffn = r_out + D_MODEL
    wdown = lambda r0, r1: wdown_ref[r0:r1, :D_MODEL]
    ga = gate_ref[0, :, :D_MODEL].astype(F32)
    gb = gate_ref[0, :, D_MODEL:].astype(F32)
    rows = min(tile, DOT_ROWS)
    dot_rows = lambda a, b: jnp.concatenate([_dot(a[r0:r0 + rows], b) for r0 in range(0, tile, rows)], axis=0)
    merged = (_sigmoid(ga) * dot_rows(yatt_ref[...], wdown(r_att, r_dn))
              + _sigmoid(gb) * dot_rows(ydn_ref[...], wdown(r_dn, r_out)))
    mixed = dot_rows(merged.astype(BF16), wdown(r_out, r_ffn))
    dn_scores = _deltanet_scores(dq_ref, dk_ref, tile)
    x = h_ref[0] + _mask_pads(_rms(mixed, nmix_ref[...]), prefix)
    hf = _rms(x, npre_ref[...]).astype(BF16)
    gate_up = _dot(hf, wfi_ref[...])
    ffn = _dot((_silu(gate_up[:, :d_ff]) * gate_up[:, d_ff:]).astype(BF16), wdown(r_ffn, r_ffn + d_ff))
    out_ref[0] = x + _mask_pads(_rms(ffn, npost_ref[...]), prefix)

    y_att = _attention_tile(sink_ref, q_ref, kv_ref, halo_ref, meta_ref, j_mix, tile, prefix)
    y_dn = _deltanet_tile(dn_scores, dq_ref, dk_ref, dv_ref, dz_ref, pack_ref, dnw_ref, state_ref, tile)
    yatt_ref[...] = y_att
    ydn_ref[...] = y_dn
    stateout_ref[...] = state_ref[...]


def _layer_tail(h, q, kv, kv_meta, dq, dk, dv, dz, pack, gates, sinks, dn_norm_w, state_init, wdown, nmix, npre,
                wfi, npost, layer, *, tile, prefix):
    B, L, D = h.shape
    tiles_per_seq = L // tile
    n_tiles = B * tiles_per_seq
    nblk = tile // BLOCK

    def mix_tile(s):
        t = jnp.minimum(s, n_tiles - 1)
        return t // tiles_per_seq, t % tiles_per_seq

    def ffn_tile(s):
        t = jnp.maximum(s - 1, 0)
        return t // tiles_per_seq, t % tiles_per_seq

    mix_rows = lambda width: pl.BlockSpec((1, tile, width), lambda s: (*mix_tile(s), 0))
    ffn_rows = lambda width: pl.BlockSpec((1, tile, width), lambda s: (*ffn_tile(s), 0))
    halo = pl.BlockSpec((1, BLOCK, 2 * ATT_KV),
                        lambda s: (mix_tile(s)[0], jnp.maximum(mix_tile(s)[1] * nblk - 1, 0), 0))
    state_shape = (DN_HEADS, DN_HEAD_DIM, DN_HEAD_DIM)
    kernel_fn = functools.partial(_tail_kernel, tile=tile, prefix=prefix, n_tiles=n_tiles,
                                  tiles_per_seq=tiles_per_seq)
    return pl.pallas_call(
        kernel_fn,
        grid=(n_tiles + 1,),
        in_specs=[pl.BlockSpec(memory_space=pltpu.SMEM),
                  mix_rows(ATT_Q), mix_rows(2 * ATT_KV), halo, _resident(kv_meta.shape),
                  mix_rows(DN_W), mix_rows(DN_W), mix_rows(DN_W), mix_rows(DN_W), mix_rows(LANES),
                  _resident(dn_norm_w.shape, layer), _resident(state_shape),
                  ffn_rows(D), ffn_rows(2 * D),
                  _resident(wdown.shape, layer), _resident(nmix.shape, layer), _resident(npre.shape, layer),
                  _resident(wfi.shape, layer), _resident(npost.shape, layer)],
        out_specs=[ffn_rows(D), pl.BlockSpec(state_shape, lambda s: (0, 0, 0))],
        out_shape=[jax.ShapeDtypeStruct((B, L, D), F32), jax.ShapeDtypeStruct(state_shape, F32)],
        scratch_shapes=[pltpu.VMEM(state_shape, F32),
                        pltpu.VMEM((tile, ATT_Q), BF16), pltpu.VMEM((tile, DN_W), BF16)],
        compiler_params=pltpu.CompilerParams(
            dimension_semantics=("arbitrary",), vmem_limit_bytes=VMEM_LIMIT),
        name="layer_tail",
    )(sinks, q, kv, kv, kv_meta, dq, dk, dv, dz, pack, dn_norm_w, state_init, h, gates, wdown, nmix, npre, wfi,
      npost)


def _regroup_in_proj(t):
    d0 = ATT_Q + 2 * ATT_KV
    z0 = d0 + 3 * DN_W
    a0 = z0 + DN_W
    g0 = a0 + 2 * DN_HEADS
    reps = (1,) * (t.ndim - 1) + (LANES // (2 * DN_HEADS),)
    return jnp.concatenate([t[..., d0:z0], t[..., :d0], jnp.tile(t[..., a0:g0], reps), t[..., z0:a0],
                            t[..., g0:]], axis=-1)


def kernel(x, meta_tokens, w_in, b_in, conv_w, a_log, dt_bias, dn_norm_w, att_sinks, w_att_out, w_dn_out,
           w_out, norm_mix_pre, norm_mix_post, norm_ffn_pre, norm_ffn_post, w_ffn_in, w_ffn_out):
    depth = w_in.shape[0]
    h_prefix = jnp.concatenate([jnp.zeros((PAD_FRONT, D_MODEL), x.dtype), meta_tokens.astype(x.dtype)])[None]
    h_main = x

    rows = lambda v: v.astype(F32)[:, None, :]
    lane_tiled = lambda v: jnp.tile(v.astype(F32), (1, LANES // DN_HEADS))[:, None, :]
    w_main = _regroup_in_proj(w_in).astype(BF16)
    b_main = rows(_regroup_in_proj(b_in))
    wdown = jnp.pad(jnp.concatenate([w_att_out, w_dn_out, w_out, w_ffn_out], axis=1).astype(BF16),
                    ((0, 0), (0, 0), (0, LANES)))
    wfi = w_ffn_in.astype(BF16)
    conv_w32, alog, dtb = conv_w.astype(F32), lane_tiled(a_log), lane_tiled(dt_bias)
    sinks = att_sinks.astype(F32)
    n_pre, n_dn, n_mix, n_ffn_pre, n_ffn_post = (rows(v) for v in (
        norm_mix_pre, dn_norm_w, norm_mix_post, norm_ffn_pre, norm_ffn_post))
    no_history = jnp.zeros((1, SUBLANES, 3 * DN_W), F32)
    no_state = jnp.zeros((DN_HEADS, DN_HEAD_DIM, DN_HEAD_DIM), F32)

    def layer(h, l, kv_meta, conv_init, state_init, tile, prefix):
        q, kv, dq, dk, dv, dz, gates, pack, conv_tail = _inproj(
            h, n_pre, w_main, b_main, conv_w32, alog, dtb, conv_init, l, tile=tile, prefix=prefix)
        h, state = _layer_tail(h, q, kv, kv if prefix else kv_meta, dq, dk, dv, dz, pack, gates, sinks[l], n_dn,
                               state_init, wdown, n_mix, n_ffn_pre, wfi, n_ffn_post, l, tile=tile, prefix=prefix)
        return h, kv, conv_tail, state

    for l in range(depth):
        h_prefix, kv_prefix, conv_prefix, state_prefix = layer(h_prefix, l, None, no_history, no_state,
                                                               PREFIX, True)
        h_main, _, _, _ = layer(h_main, l, kv_prefix, conv_prefix, state_prefix, TILE, False)
    return h_main
```

```python
import functools

import jax
import jax.numpy as jnp
from jax import lax
from jax.experimental import pallas as pl
from jax.experimental.pallas import tpu as pltpu

D_MODEL = 1024
N_META = 16
BLOCK = 128
PREFIX = BLOCK
PAD_FRONT = PREFIX - N_META
ATT_HEADS = 8
ATT_KV_HEADS = 2
ATT_HEAD_DIM = 64
ATT_GROUP = ATT_HEADS // ATT_KV_HEADS
ATT_Q = ATT_HEADS * ATT_HEAD_DIM
ATT_KV = ATT_KV_HEADS * ATT_HEAD_DIM
DN_HEADS = 4
DN_HEAD_DIM = 128
DN_W = DN_HEADS * DN_HEAD_DIM
CONV_WIDTH = 4
RMS_EPS = 1e-6
DN_CHUNK = 128
DN_BASE = 16
LANES = 128
SUBLANES = 8
TILE = 512

PACK_GCUM, PACK_BETA, PACK_EG, PACK_EDEC, PACK_EGL = 0, 1, 2, 4, 6

F32 = jnp.float32
BF16 = jnp.bfloat16

VMEM_LIMIT = 56 * 1024 * 1024


def _resident(shape, layer=None):
    if layer is None:
        return pl.BlockSpec(shape, lambda *_: (0,) * len(shape), pipeline_mode=pl.Buffered(1))
    nd = len(shape) - 1
    return pl.BlockSpec((None,) + tuple(shape[1:]), lambda *_: (layer,) + (0,) * nd,
                        pipeline_mode=pl.Buffered(1))


def _rms(x, w):
    ms = jnp.mean(x * x, axis=-1, keepdims=True)
    return x * lax.rsqrt(ms + RMS_EPS) * w


def _sigmoid(x):
    return 0.5 * jnp.tanh(0.5 * x) + 0.5


def _silu(x):
    half = 0.5 * x
    return half * jnp.tanh(half) + half


def _dot(a, b):
    return jnp.dot(a, b, preferred_element_type=F32)


def _dot_nt(a, b):
    return lax.dot_general(a, b, (((1,), (1,)), ((), ())), preferred_element_type=F32)


def _dot_tn(a, b):
    return lax.dot_general(a, b, (((0,), (0,)), ((), ())), preferred_element_type=F32)


def _iota2(shape, dim):
    return lax.broadcasted_iota(jnp.int32, shape, dim)


def _mask_pads(r, prefix):
    if not prefix:
        return r
    return jnp.concatenate([r[:PAD_FRONT] * 0.0, r[PAD_FRONT:]], axis=0)


IN_DQKV = 0
IN_Q = IN_DQKV + 3 * DN_W
IN_KV = IN_Q + ATT_Q
IN_LOGITS = IN_KV + 2 * ATT_KV
IN_DZ = IN_LOGITS + LANES
IN_GATE = IN_DZ + DN_W
IN_MAIN = IN_GATE + 2 * D_MODEL
DOT_ROWS = 256


def _inproj_kernel(x_ref, nw_ref, w_ref, b_ref, convw_ref, alog_ref, dtb_ref, convinit_ref,
                   q_ref, kv_ref, dq_ref, dk_ref, dv_ref, dz_ref, gate_ref, pack_ref, convtail_ref, ext_ref,
                   *, tile, prefix):
    j = pl.program_id(1)

    @pl.when(j == 0)
    def _():
        ext_ref[0:SUBLANES, :] = convinit_ref[0]

    hn = _rms(x_ref[0], nw_ref[...]).astype(BF16)

    def proj(c0, c1):
        rows = min(tile, DOT_ROWS)
        parts = [_dot(hn[r0:r0 + rows], w_ref[:, c0:c1]) + b_ref[:, c0:c1] for r0 in range(0, tile, rows)]
        return _mask_pads(jnp.concatenate(parts, axis=0), prefix)

    def after(v):
        bits = pltpu.bitcast(v[:SUBLANES, :LANES], jnp.uint32)
        return pltpu.bitcast(lax.shift_right_logical(bits, jnp.uint32(32)), F32)[:1]

    def conv_group(i, anchor):
        cols = slice(i * DN_HEAD_DIM, (i + 1) * DN_HEAD_DIM)
        x = ext_ref[:, cols]
        taps = [convw_ref[t:t + 1, cols] + anchor for t in range(CONV_WIDTH)]
        acc = x[SUBLANES:] * taps[CONV_WIDTH - 1]
        for shift in range(1, CONV_WIDTH):
            acc = acc + pltpu.roll(x, shift, 0)[SUBLANES:] * taps[CONV_WIDTH - 1 - shift]
        y = _silu(acc)
        kind, head = divmod(i, DN_HEADS)
        lanes = slice(head * DN_HEAD_DIM, (head + 1) * DN_HEAD_DIM)
        if kind == 2:
            dv_ref[0, :, lanes] = y.astype(BF16)
        else:
            scale = lax.rsqrt(jnp.sum(y * y, axis=-1, keepdims=True) + RMS_EPS)
            if kind == 0:
                dq_ref[0, :, lanes] = (y * (scale * (DN_HEAD_DIM ** -0.5))).astype(BF16)
            else:
                dk_ref[0, :, lanes] = (y * scale).astype(BF16)

    r_dqkv = proj(IN_DQKV, IN_Q)
    ext_ref[SUBLANES:, :] = r_dqkv
    r_mid = proj(IN_Q, IN_GATE)
    q_ref[0] = (r_mid[:, :ATT_Q] * (ATT_HEAD_DIM ** -0.5)).astype(BF16)
    kv_ref[0] = r_mid[:, IN_KV - IN_Q:IN_LOGITS - IN_Q].astype(BF16)
    logits = r_mid[:, IN_LOGITS - IN_Q:IN_DZ - IN_Q]
    dz_ref[0] = r_mid[:, IN_DZ - IN_Q:].astype(BF16)
    gate_ref[0] = proj(IN_GATE, IN_MAIN).astype(BF16)
    for i in range(3 * DN_HEADS):
        conv_group(i, after(r_mid))
    tail = ext_ref[tile:tile + SUBLANES, :]
    ext_ref[0:SUBLANES, :] = tail
    convtail_ref[0] = tail

    g = -jnp.exp(alog_ref[...]) * jax.nn.softplus(logits + dtb_ref[...])
    beta = _sigmoid(logits)
    C = DN_CHUNK
    ri = _iota2((C, C), 0)
    ci = _iota2((C, C), 1)
    sums = jnp.concatenate([(ri >= ci).astype(BF16), jnp.ones((C, C), BF16)], axis=0)
    g_hi = g.astype(BF16)
    g_lo = (g - g_hi.astype(F32)).astype(BF16)
    both = [_dot(sums, g_hi[r0:r0 + C]) + _dot(sums, g_lo[r0:r0 + C]) for r0 in range(0, tile, C)]
    gcum = jnp.concatenate([b[:C] for b in both], axis=0)
    gl = jnp.concatenate([b[C:] for b in both], axis=0)
    grp = _iota2((tile, LANES), 1) // DN_HEADS
    pack = jnp.where(grp == PACK_GCUM, gcum,
                     jnp.where(grp == PACK_BETA, beta,
                               jnp.where(grp == PACK_EG, jnp.exp(gcum),
                                         jnp.where(grp == PACK_EDEC, jnp.exp(gl - gcum), jnp.exp(gl)))))
    pack_ref[0] = pack


def _inproj(h, nw, w, b, conv_w, alog, dtb, conv_init, layer, *, tile, prefix):
    B, L, D = h.shape
    rows = lambda width: pl.BlockSpec((1, tile, width), lambda b, j: (b, j, 0))
    widths = (ATT_Q, 2 * ATT_KV, DN_W, DN_W, DN_W, DN_W, 2 * D_MODEL, LANES)
    dtypes = (BF16,) * 7 + (F32,)
    conv_rows = lambda imap: pl.BlockSpec((1, SUBLANES, 3 * DN_W), imap)
    return pl.pallas_call(
        functools.partial(_inproj_kernel, tile=tile, prefix=prefix),
        grid=(B, L // tile),
        in_specs=[rows(D), _resident(nw.shape, layer), _resident(w.shape, layer), _resident(b.shape, layer),
                  _resident(conv_w.shape, layer), _resident(alog.shape, layer), _resident(dtb.shape, layer),
                  conv_rows(lambda b, j: (0, 0, 0))],
        out_specs=[rows(width) for width in widths] + [conv_rows(lambda b, j: (b, 0, 0))],
        out_shape=([jax.ShapeDtypeStruct((B, L, width), dt) for width, dt in zip(widths, dtypes)]
                   + [jax.ShapeDtypeStruct((B, SUBLANES, 3 * DN_W), F32)]),
        scratch_shapes=[pltpu.VMEM((tile + SUBLANES, 3 * DN_W), F32)],
        compiler_params=pltpu.CompilerParams(
            dimension_semantics=("parallel", "arbitrary"), vmem_limit_bytes=VMEM_LIMIT),
        name="inproj",
    )(h, nw, w, b, conv_w, alog, dtb, conv_init)


ATT_ROWS = ATT_GROUP * BLOCK


def _attn_blocks(blocks, kv_meta, fills, prefix):
    qi = _iota2((ATT_ROWS, BLOCK), 0) % BLOCK
    kj = _iota2((ATT_ROWS, BLOCK), 1)
    upper = kj > qi
    meta_ok = kj >= PAD_FRONT
    if prefix:
        meta_ok = meta_ok & (kj <= qi)
    neg_inf = jnp.float32(-jnp.inf)

    chains = [(blk, g) for blk in blocks for g in range(ATT_KV_HEADS)]
    ksl = lambda g: slice(g * ATT_HEAD_DIM, (g + 1) * ATT_HEAD_DIM)
    vsl = lambda g: slice(ATT_KV + g * ATT_HEAD_DIM, ATT_KV + (g + 1) * ATT_HEAD_DIM)
    zeros = jnp.zeros((BLOCK, ATT_HEAD_DIM), BF16)
    ones = jnp.ones((BLOCK, ATT_HEAD_DIM), BF16)

    s_all = []
    for (q, kv_prev, kv_cur, _), g in chains:
        qg = jnp.concatenate([q[:, h * ATT_HEAD_DIM:(h + 1) * ATT_HEAD_DIM]
                              for h in range(g * ATT_GROUP, (g + 1) * ATT_GROUP)], axis=0)
        keys = jnp.concatenate([kv_prev[:, ksl(g)], kv_cur[:, ksl(g)], kv_meta[:, ksl(g)]], axis=0)
        s_all.append(_dot_nt(qg, keys))
    v_ext = []
    for (_, kv_prev, kv_cur, _), g in chains:
        v_diff = (kv_prev[:, vsl(g)].astype(F32) - kv_cur[:, vsl(g)].astype(F32)).astype(BF16)
        v_ext.append(jnp.concatenate([jnp.concatenate([kv_cur[:, vsl(g)], ones], axis=1),
                                      jnp.concatenate([v_diff, zeros], axis=1),
                                      jnp.concatenate([kv_meta[:, vsl(g)], ones], axis=1)], axis=0))

    s_band, s_meta = [], []
    for ((_, _, _, n), g), s in zip(chains, s_all):
        if prefix:
            s_band.append(jnp.full((ATT_ROWS, BLOCK), neg_inf))
        else:
            s_band.append(jnp.where(upper, s[:, :BLOCK] + jnp.where(n >= 1, 0.0, neg_inf), s[:, BLOCK:2 * BLOCK]))
        s_meta.append(jnp.where(meta_ok, s[:, 2 * BLOCK:], fills[g]))
    m = [jnp.max(jnp.maximum(sb, sm), axis=-1, keepdims=True) for sb, sm in zip(s_band, s_meta)]
    p_all = []
    for sb, sm, mx in zip(s_band, s_meta, m):
        p_band = jnp.exp(sb - mx)
        p_all.append(jnp.concatenate([p_band.astype(BF16), jnp.where(upper, p_band, 0.0).astype(BF16),
                                      jnp.exp(sm - mx).astype(BF16)], axis=1))
    o_ext = [_dot(p, v) for p, v in zip(p_all, v_ext)]
    o = [(x * pltpu.roll(1.0 / x, ATT_HEAD_DIM, 1))[:, :ATT_HEAD_DIM] for x in o_ext]

    outs = []
    for b in range(len(blocks)):
        heads = [o[b * ATT_KV_HEADS + g][i * BLOCK:(i + 1) * BLOCK]
                 for g in range(ATT_KV_HEADS) for i in range(ATT_GROUP)]
        outs.append(jnp.concatenate(heads, axis=1).astype(BF16))
    return outs


def _attention_tile(sink_ref, q_ref, kv_ref, halo_ref, meta_ref, j, tile, prefix):
    nblk = tile // BLOCK
    lane = _iota2((BLOCK, BLOCK), 1)
    fills = [jnp.concatenate([jnp.where(lane == 0, sink_ref[g * ATT_GROUP + i], -jnp.inf)
                              for i in range(ATT_GROUP)], axis=0) for g in range(ATT_KV_HEADS)]
    rows = lambda i: slice(i * BLOCK, (i + 1) * BLOCK)
    blocks = [(q_ref[0, rows(i), :], halo_ref[0] if i == 0 else kv_ref[0, rows(i - 1), :],
               kv_ref[0, rows(i), :], j * nblk + i) for i in range(nblk)]
    return jnp.concatenate(_attn_blocks(blocks, meta_ref[0], fills, prefix), axis=0)


def _inverse_masks():
    C = DN_CHUNK
    ri = _iota2((C, C), 0)
    ci = _iota2((C, C), 1)
    same = lambda size: (ri // size) == (ci // size)
    levels = []
    size = DN_BASE
    while size < C:
        levels.append(same(2 * size) & ~same(size))
        size *= 2
    return (ri == ci).astype(F32), same(DN_BASE), levels


def _unit_lower_inverses(mats, masks):
    C = DN_CHUNK
    eye, base, levels = masks
    ds = [jnp.where(base, a, 0.0) for a in mats]
    ts = [eye - d for d in ds]
    d16 = [d.astype(BF16) for d in ds]
    powers = [_dot(d, d) for d in d16]
    size = 2
    while size < DN_BASE:
        pbs = [p.astype(BF16) for p in powers]
        if 2 * size < DN_BASE:
            both = [_dot(jnp.concatenate([t.astype(BF16), pb], axis=0), pb) for t, pb in zip(ts, pbs)]
            ts = [t + b[:C] for t, b in zip(ts, both)]
            powers = [b[C:] for b in both]
        else:
            ts = [t + _dot(t.astype(BF16), pb) for t, pb in zip(ts, pbs)]
        size *= 2
    for level in levels:
        es = [jnp.where(level, a, 0.0).astype(BF16) for a in mats]
        tbs = [t.astype(BF16) for t in ts]
        tes = [_dot(tb, e).astype(BF16) for tb, e in zip(tbs, es)]
        ts = [t - _dot(te, tb) for t, te, tb in zip(ts, tes, tbs)]
    return ts


def _deltanet_scores(q_ref, k_ref, tile):
    C = DN_CHUNK
    chains = [(c, h) for c in range(tile // C) for h in range(DN_HEADS)]
    q16 = [q_ref[0, c * C:(c + 1) * C, h * DN_HEAD_DIM:(h + 1) * DN_HEAD_DIM] for c, h in chains]
    k16 = [k_ref[0, c * C:(c + 1) * C, h * DN_HEAD_DIM:(h + 1) * DN_HEAD_DIM] for c, h in chains]
    return [_dot_nt(k, k) for k in k16], [_dot_nt(q, k) for q, k in zip(q16, k16)]


def _deltanet_tile(scores, q_ref, k_ref, v_ref, z_ref, pack_ref, nw_ref, state_ref, tile):
    C = DN_CHUNK
    nchunk = tile // C

    ri = _iota2((C, C), 0)
    ci = _iota2((C, C), 1)
    tril = ri >= ci
    strict = ri > ci
    masks = _inverse_masks()

    chains = [(c, h) for c in range(nchunk) for h in range(DN_HEADS)]
    rows = lambda c: slice(c * C, (c + 1) * C)
    lanes = lambda h: slice(h * DN_HEAD_DIM, (h + 1) * DN_HEAD_DIM)
    packs = [pack_ref[0, rows(c), :] for c in range(nchunk)]
    pack_ts = [p.T for p in packs]
    col = lambda c, h, grp: packs[c][:, DN_HEADS * grp + h:DN_HEADS * grp + h + 1]
    row = lambda c, h, grp: pack_ts[c][DN_HEADS * grp + h:DN_HEADS * grp + h + 1, :]

    q16 = [q_ref[0, rows(c), lanes(h)] for c, h in chains]
    k16 = [k_ref[0, rows(c), lanes(h)] for c, h in chains]
    v16 = [v_ref[0, rows(c), lanes(h)] for c, h in chains]
    decay = [jnp.exp(jnp.where(tril, col(c, h, PACK_GCUM) - row(c, h, PACK_GCUM), -jnp.inf)) for c, h in chains]
    kk, qk = scores
    mats = [jnp.where(strict, x * d, 0.0) * col(c, h, PACK_BETA) for x, d, (c, h) in zip(kk, decay, chains)]
    attn = [(x * d).astype(BF16) for x, d in zip(qk, decay)]
    ts = _unit_lower_inverses(mats, masks)
    tb = [(t * row(c, h, PACK_BETA)).astype(BF16) for t, (c, h) in zip(ts, chains)]
    k_eg = [(k.astype(F32) * col(c, h, PACK_EG)).astype(BF16) for k, (c, h) in zip(k16, chains)]
    uw = [_dot(t, jnp.concatenate([v, ke], axis=1)).astype(BF16) for t, v, ke in zip(tb, v16, k_eg)]
    q_dec = [q.astype(F32) * col(c, h, PACK_EG) for q, (c, h) in zip(q16, chains)]
    k_dec = [(k.astype(F32) * col(c, h, PACK_EDEC)).astype(BF16) for k, (c, h) in zip(k16, chains)]
    ktuw = [_dot_tn(kd, x) for kd, x in zip(k_dec, uw)]
    auw = [_dot(a, x) for a, x in zip(attn, uw)]
    lhs = [jnp.concatenate([-kt[:, DN_HEAD_DIM:], qd - a[:, DN_HEAD_DIM:]], axis=0).astype(BF16)
           for kt, qd, a in zip(ktuw, q_dec, auw)]

    ys = []
    for c in range(nchunk):
        idx = [c * DN_HEADS + h for h in range(DN_HEADS)]
        states = [state_ref[h] for h in range(DN_HEADS)]
        both = [_dot(lhs[i], s.astype(BF16)) for i, s in zip(idx, states)]
        for h, (i, s, b) in enumerate(zip(idx, states, both)):
            egl = packs[c][C - 1:C, DN_HEADS * PACK_EGL + h:DN_HEADS * PACK_EGL + h + 1]
            state_ref[h] = s * egl + b[:DN_HEAD_DIM] + ktuw[i][:, :DN_HEAD_DIM]
        outs = [b[DN_HEAD_DIM:] + auw[i][:, :DN_HEAD_DIM] for i, b in zip(idx, both)]
        ys.append(jnp.concatenate([(_rms(o, nw_ref[...]) * _silu(z_ref[0, rows(c), lanes(h)].astype(F32))
                                    ).astype(BF16) for h, o in enumerate(outs)], axis=1))
    return jnp.concatenate(ys, axis=0)


def _tail_kernel(sink_ref, q_ref, kv_ref, halo_ref, meta_ref, dq_ref, dk_ref, dv_ref, dz_ref, pack_ref,
                 dnw_ref, stateinit_ref, h_ref, gate_ref, wdown_ref, nmix_ref, npre_ref, wfi_ref, npost_ref,
                 out_ref, stateout_ref, state_ref, yatt_ref, ydn_ref,
                 *, tile, prefix, n_tiles, tiles_per_seq):
    s = pl.program_id(0)
    j_mix = jnp.minimum(s, n_tiles - 1) % tiles_per_seq

    @pl.when(s == 0)
    def _():
        yatt_ref[...] = jnp.zeros_like(yatt_ref)
        ydn_ref[...] = jnp.zeros_like(ydn_ref)

    @pl.when(j_mix == 0)
    def _():
        state_ref[...] = stateinit_ref[...]

    d_ff = wfi_ref.shape[1] // 2
    r_att, r_dn, r_out = 0, ATT_Q, ATT_Q + DN_W
    r_ffn = r_out + D_MODEL
    wdown = lambda r0, r1: wdown_ref[r0:r1, :D_MODEL]
    ga = gate_ref[0, :, :D_MODEL].astype(F32)
    gb = gate_ref[0, :, D_MODEL:].astype(F32)
    rows = min(tile, DOT_ROWS)
    dot_rows = lambda a, b: jnp.concatenate([_dot(a[r0:r0 + rows], b) for r0 in range(0, tile, rows)], axis=0)
    merged = (_sigmoid(ga) * dot_rows(yatt_ref[...], wdown(r_att, r_dn))
              + _sigmoid(gb) * dot_rows(ydn_ref[...], wdown(r_dn, r_out)))
    mixed = dot_rows(merged.astype(BF16), wdown(r_out, r_ffn))
    dn_scores = _deltanet_scores(dq_ref, dk_ref, tile)
    x = h_ref[0] + _mask_pads(_rms(mixed, nmix_ref[...]), prefix)
    hf = _rms(x, npre_ref[...]).astype(BF16)
    gate_up = dot_rows(hf, wfi_ref[...])
    ffn = dot_rows((_silu(gate_up[:, :d_ff]) * gate_up[:, d_ff:]).astype(BF16), wdown(r_ffn, r_ffn + d_ff))
    out_ref[0] = x + _mask_pads(_rms(ffn, npost_ref[...]), prefix)

    y_att = _attention_tile(sink_ref, q_ref, kv_ref, halo_ref, meta_ref, j_mix, tile, prefix)
    y_dn = _deltanet_tile(dn_scores, dq_ref, dk_ref, dv_ref, dz_ref, pack_ref, dnw_ref, state_ref, tile)
    yatt_ref[...] = y_att
    ydn_ref[...] = y_dn
    stateout_ref[...] = state_ref[...]


def _layer_tail(h, q, kv, kv_meta, dq, dk, dv, dz, pack, gates, sinks, dn_norm_w, state_init, wdown, nmix, npre,
                wfi, npost, layer, *, tile, prefix):
    B, L, D = h.shape
    tiles_per_seq = L // tile
    n_tiles = B * tiles_per_seq
    nblk = tile // BLOCK

    def mix_tile(s):
        t = jnp.minimum(s, n_tiles - 1)
        return t // tiles_per_seq, t % tiles_per_seq

    def ffn_tile(s):
        t = jnp.maximum(s - 1, 0)
        return t // tiles_per_seq, t % tiles_per_seq

    mix_rows = lambda width: pl.BlockSpec((1, tile, width), lambda s: (*mix_tile(s), 0))
    ffn_rows = lambda width: pl.BlockSpec((1, tile, width), lambda s: (*ffn_tile(s), 0))
    halo = pl.BlockSpec((1, BLOCK, 2 * ATT_KV),
                        lambda s: (mix_tile(s)[0], jnp.maximum(mix_tile(s)[1] * nblk - 1, 0), 0))
    state_shape = (DN_HEADS, DN_HEAD_DIM, DN_HEAD_DIM)
    kernel_fn = functools.partial(_tail_kernel, tile=tile, prefix=prefix, n_tiles=n_tiles,
                                  tiles_per_seq=tiles_per_seq)
    return pl.pallas_call(
        kernel_fn,
        grid=(n_tiles + 1,),
        in_specs=[pl.BlockSpec(memory_space=pltpu.SMEM),
                  mix_rows(ATT_Q), mix_rows(2 * ATT_KV), halo, _resident(kv_meta.shape),
                  mix_rows(DN_W), mix_rows(DN_W), mix_rows(DN_W), mix_rows(DN_W), mix_rows(LANES),
                  _resident(dn_norm_w.shape, layer), _resident(state_shape),
                  ffn_rows(D), ffn_rows(2 * D),
                  _resident(wdown.shape, layer), _resident(nmix.shape, layer), _resident(npre.shape, layer),
                  _resident(wfi.shape, layer), _resident(npost.shape, layer)],
        out_specs=[ffn_rows(D), pl.BlockSpec(state_shape, lambda s: (0, 0, 0))],
        out_shape=[jax.ShapeDtypeStruct((B, L, D), F32), jax.ShapeDtypeStruct(state_shape, F32)],
        scratch_shapes=[pltpu.VMEM(state_shape, F32),
                        pltpu.VMEM((tile, ATT_Q), BF16), pltpu.VMEM((tile, DN_W), BF16)],
        compiler_params=pltpu.CompilerParams(
            dimension_semantics=("arbitrary",), vmem_limit_bytes=VMEM_LIMIT),
        name="layer_tail",
    )(sinks, q, kv, kv, kv_meta, dq, dk, dv, dz, pack, dn_norm_w, state_init, h, gates, wdown, nmix, npre, wfi,
      npost)


def _regroup_in_proj(t):
    d0 = ATT_Q + 2 * ATT_KV
    z0 = d0 + 3 * DN_W
    a0 = z0 + DN_W
    g0 = a0 + 2 * DN_HEADS
    reps = (1,) * (t.ndim - 1) + (LANES // (2 * DN_HEADS),)
    return jnp.concatenate([t[..., d0:z0], t[..., :d0], jnp.tile(t[..., a0:g0], reps), t[..., z0:a0],
                            t[..., g0:]], axis=-1)


def kernel(x, meta_tokens, w_in, b_in, conv_w, a_log, dt_bias, dn_norm_w, att_sinks, w_att_out, w_dn_out,
           w_out, norm_mix_pre, norm_mix_post, norm_ffn_pre, norm_ffn_post, w_ffn_in, w_ffn_out):
    depth = w_in.shape[0]
    h_prefix = jnp.concatenate([jnp.zeros((PAD_FRONT, D_MODEL), x.dtype), meta_tokens.astype(x.dtype)])[None]
    h_main = x

    rows = lambda v: v.astype(F32)[:, None, :]
    lane_tiled = lambda v: jnp.tile(v.astype(F32), (1, LANES // DN_HEADS))[:, None, :]
    w_main = _regroup_in_proj(w_in).astype(BF16)
    b_main = rows(_regroup_in_proj(b_in))
    wdown = jnp.pad(jnp.concatenate([w_att_out, w_dn_out, w_out, w_ffn_out], axis=1).astype(BF16),
                    ((0, 0), (0, 0), (0, LANES)))
    wfi = w_ffn_in.astype(BF16)
    conv_w32, alog, dtb = conv_w.astype(F32), lane_tiled(a_log), lane_tiled(dt_bias)
    sinks = att_sinks.astype(F32)
    n_pre, n_dn, n_mix, n_ffn_pre, n_ffn_post = (rows(v) for v in (
        norm_mix_pre, dn_norm_w, norm_mix_post, norm_ffn_pre, norm_ffn_post))
    no_history = jnp.zeros((1, SUBLANES, 3 * DN_W), F32)
    no_state = jnp.zeros((DN_HEADS, DN_HEAD_DIM, DN_HEAD_DIM), F32)

    def layer(h, l, kv_meta, conv_init, state_init, tile, prefix):
        q, kv, dq, dk, dv, dz, gates, pack, conv_tail = _inproj(
            h, n_pre, w_main, b_main, conv_w32, alog, dtb, conv_init, l, tile=tile, prefix=prefix)
        h, state = _layer_tail(h, q, kv, kv if prefix else kv_meta, dq, dk, dv, dz, pack, gates, sinks[l], n_dn,
                               state_init, wdown, n_mix, n_ffn_pre, wfi, n_ffn_post, l, tile=tile, prefix=prefix)
        return h, kv, conv_tail, state

    for l in range(depth):
        h_prefix, kv_prefix, conv_prefix, state_prefix = layer(h_prefix, l, None, no_history, no_state,
                                                               PREFIX, True)
        h_main, _, _, _ = layer(h_main, l, kv_prefix, conv_prefix, state_prefix, TILE, False)
    return h_main
```

```python
import functools

import jax
import jax.numpy as jnp
from jax import lax
from jax.experimental import pallas as pl
from jax.experimental.pallas import tpu as pltpu

D_MODEL = 1024
N_META = 16
BLOCK = 128
PREFIX = BLOCK
PAD_FRONT = PREFIX - N_META
ATT_HEADS = 8
ATT_KV_HEADS = 2
ATT_HEAD_DIM = 64
ATT_GROUP = ATT_HEADS // ATT_KV_HEADS
ATT_Q = ATT_HEADS * ATT_HEAD_DIM
ATT_KV = ATT_KV_HEADS * ATT_HEAD_DIM
DN_HEADS = 4
DN_HEAD_DIM = 128
DN_W = DN_HEADS * DN_HEAD_DIM
CONV_WIDTH = 4
RMS_EPS = 1e-6
DN_CHUNK = 128
DN_BASE = 16
LANES = 128
SUBLANES = 8
TILE = 512

PACK_GCUM, PACK_BETA, PACK_EG, PACK_EDEC, PACK_EGL = 0, 1, 2, 4, 6

F32 = jnp.float32
BF16 = jnp.bfloat16

VMEM_LIMIT = 56 * 1024 * 1024


def _resident(shape, layer=None):
    if layer is None:
        return pl.BlockSpec(shape, lambda *_: (0,) * len(shape), pipeline_mode=pl.Buffered(1))
    nd = len(shape) - 1
    return pl.BlockSpec((None,) + tuple(shape[1:]), lambda *_: (layer,) + (0,) * nd,
                        pipeline_mode=pl.Buffered(1))


def _rms(x, w):
    ms = jnp.mean(x * x, axis=-1, keepdims=True)
    return x * lax.rsqrt(ms + RMS_EPS) * w


def _sigmoid(x):
    return 0.5 * jnp.tanh(0.5 * x) + 0.5


def _silu(x):
    half = 0.5 * x
    return half * jnp.tanh(half) + half


def _dot(a, b):
    return jnp.dot(a, b, preferred_element_type=F32)


def _dot_nt(a, b):
    return lax.dot_general(a, b, (((1,), (1,)), ((), ())), preferred_element_type=F32)


def _dot_tn(a, b):
    return lax.dot_general(a, b, (((0,), (0,)), ((), ())), preferred_element_type=F32)


def _iota2(shape, dim):
    return lax.broadcasted_iota(jnp.int32, shape, dim)


def _mask_pads(r, prefix):
    if not prefix:
        return r
    return jnp.concatenate([r[:PAD_FRONT] * 0.0, r[PAD_FRONT:]], axis=0)


IN_DQKV = 0
IN_Q = IN_DQKV + 3 * DN_W
IN_KV = IN_Q + ATT_Q
IN_LOGITS = IN_KV + 2 * ATT_KV
IN_DZ = IN_LOGITS + LANES
IN_GATE = IN_DZ + DN_W
IN_MAIN = IN_GATE + 2 * D_MODEL
DOT_ROWS = 256


def _inproj_kernel(x_ref, nw_ref, w_ref, b_ref, convw_ref, alog_ref, dtb_ref, convinit_ref,
                   q_ref, kv_ref, dq_ref, dk_ref, dv_ref, dz_ref, gate_ref, pack_ref, convtail_ref, ext_ref,
                   *, tile, prefix):
    j = pl.program_id(1)

    @pl.when(j == 0)
    def _():
        ext_ref[0:SUBLANES, :] = convinit_ref[0]

    hn = _rms(x_ref[0], nw_ref[...]).astype(BF16)

    def proj(c0, c1):
        rows = min(tile, DOT_ROWS)
        parts = [_dot(hn[r0:r0 + rows], w_ref[:, c0:c1]) + b_ref[:, c0:c1] for r0 in range(0, tile, rows)]
        return _mask_pads(jnp.concatenate(parts, axis=0), prefix)

    def after(v):
        bits = pltpu.bitcast(v[:SUBLANES, :LANES], jnp.uint32)
        return pltpu.bitcast(lax.shift_right_logical(bits, jnp.uint32(32)), F32)[:1]

    def conv_group(i, anchor):
        cols = slice(i * DN_HEAD_DIM, (i + 1) * DN_HEAD_DIM)
        x = ext_ref[:, cols]
        taps = [convw_ref[t:t + 1, cols] + anchor for t in range(CONV_WIDTH)]
        acc = x[SUBLANES:] * taps[CONV_WIDTH - 1]
        for shift in range(1, CONV_WIDTH):
            acc = acc + pltpu.roll(x, shift, 0)[SUBLANES:] * taps[CONV_WIDTH - 1 - shift]
        y = _silu(acc)
        kind, head = divmod(i, DN_HEADS)
        lanes = slice(head * DN_HEAD_DIM, (head + 1) * DN_HEAD_DIM)
        if kind == 2:
            dv_ref[0, :, lanes] = y.astype(BF16)
        else:
            scale = lax.rsqrt(jnp.sum(y * y, axis=-1, keepdims=True) + RMS_EPS)
            if kind == 0:
                dq_ref[0, :, lanes] = (y * (scale * (DN_HEAD_DIM ** -0.5))).astype(BF16)
            else:
                dk_ref[0, :, lanes] = (y * scale).astype(BF16)

    r_dqkv = proj(IN_DQKV, IN_Q)
    ext_ref[SUBLANES:, :] = r_dqkv
    r_mid = proj(IN_Q, IN_GATE)
    q_ref[0] = (r_mid[:, :ATT_Q] * (ATT_HEAD_DIM ** -0.5)).astype(BF16)
    kv_ref[0] = r_mid[:, IN_KV - IN_Q:IN_LOGITS - IN_Q].astype(BF16)
    logits = r_mid[:, IN_LOGITS - IN_Q:IN_DZ - IN_Q]
    dz_ref[0] = r_mid[:, IN_DZ - IN_Q:].astype(BF16)
    gate_ref[0] = proj(IN_GATE, IN_MAIN).astype(BF16)
    for i in range(3 * DN_HEADS):
        conv_group(i, after(r_mid))
    tail = ext_ref[tile:tile + SUBLANES, :]
    ext_ref[0:SUBLANES, :] = tail
    convtail_ref[0] = tail

    g = -jnp.exp(alog_ref[...]) * jax.nn.softplus(logits + dtb_ref[...])
    beta = _sigmoid(logits)
    C = DN_CHUNK
    ri = _iota2((C, C), 0)
    ci = _iota2((C, C), 1)
    sums = jnp.concatenate([(ri >= ci).astype(BF16), jnp.ones((C, C), BF16)], axis=0)
    g_hi = g.astype(BF16)
    g_lo = (g - g_hi.astype(F32)).astype(BF16)
    both = [_dot(sums, g_hi[r0:r0 + C]) + _dot(sums, g_lo[r0:r0 + C]) for r0 in range(0, tile, C)]
    gcum = jnp.concatenate([b[:C] for b in both], axis=0)
    gl = jnp.concatenate([b[C:] for b in both], axis=0)
    grp = _iota2((tile, LANES), 1) // DN_HEADS
    pack = jnp.where(grp == PACK_GCUM, gcum,
                     jnp.where(grp == PACK_BETA, beta,
                               jnp.where(grp == PACK_EG, jnp.exp(gcum),
                                         jnp.where(grp == PACK_EDEC, jnp.exp(gl - gcum), jnp.exp(gl)))))
    pack_ref[0] = pack


def _inproj(h, nw, w, b, conv_w, alog, dtb, conv_init, layer, *, tile, prefix):
    B, L, D = h.shape
    rows = lambda width: pl.BlockSpec((1, tile, width), lambda b, j: (b, j, 0))
    widths = (ATT_Q, 2 * ATT_KV, DN_W, DN_W, DN_W, DN_W, 2 * D_MODEL, LANES)
    dtypes = (BF16,) * 7 + (F32,)
    conv_rows = lambda imap: pl.BlockSpec((1, SUBLANES, 3 * DN_W), imap)
    return pl.pallas_call(
        functools.partial(_inproj_kernel, tile=tile, prefix=prefix),
        grid=(B, L // tile),
        in_specs=[rows(D), _resident(nw.shape, layer), _resident(w.shape, layer), _resident(b.shape, layer),
                  _resident(conv_w.shape, layer), _resident(alog.shape, layer), _resident(dtb.shape, layer),
                  conv_rows(lambda b, j: (0, 0, 0))],
        out_specs=[rows(width) for width in widths] + [conv_rows(lambda b, j: (b, 0, 0))],
        out_shape=([jax.ShapeDtypeStruct((B, L, width), dt) for width, dt in zip(widths, dtypes)]
                   + [jax.ShapeDtypeStruct((B, SUBLANES, 3 * DN_W), F32)]),
        scratch_shapes=[pltpu.VMEM((tile + SUBLANES, 3 * DN_W), F32)],
        compiler_params=pltpu.CompilerParams(
            dimension_semantics=("parallel", "arbitrary"), vmem_limit_bytes=VMEM_LIMIT),
        name="inproj",
    )(h, nw, w, b, conv_w, alog, dtb, conv_init)


ATT_ROWS = ATT_GROUP * BLOCK


def _attn_blocks(blocks, kv_meta, fills, prefix):
    qi = _iota2((ATT_ROWS, BLOCK), 0) % BLOCK
    kj = _iota2((ATT_ROWS, BLOCK), 1)
    upper = kj > qi
    meta_ok = kj >= PAD_FRONT
    if prefix:
        meta_ok = meta_ok & (kj <= qi)
    neg_inf = jnp.float32(-jnp.inf)

    chains = [(blk, g) for blk in blocks for g in range(ATT_KV_HEADS)]
    ksl = lambda g: slice(g * ATT_HEAD_DIM, (g + 1) * ATT_HEAD_DIM)
    vsl = lambda g: slice(ATT_KV + g * ATT_HEAD_DIM, ATT_KV + (g + 1) * ATT_HEAD_DIM)
    zeros = jnp.zeros((BLOCK, ATT_HEAD_DIM), BF16)
    ones = jnp.ones((BLOCK, ATT_HEAD_DIM), BF16)

    s_all = []
    for (q, kv_prev, kv_cur, _), g in chains:
        qg = jnp.concatenate([q[:, h * ATT_HEAD_DIM:(h + 1) * ATT_HEAD_DIM]
                              for h in range(g * ATT_GROUP, (g + 1) * ATT_GROUP)], axis=0)
        keys = jnp.concatenate([kv_prev[:, ksl(g)], kv_cur[:, ksl(g)], kv_meta[:, ksl(g)]], axis=0)
        s_all.append(_dot_nt(qg, keys))
    v_ext = []
    for (_, kv_prev, kv_cur, _), g in chains:
        v_diff = (kv_prev[:, vsl(g)].astype(F32) - kv_cur[:, vsl(g)].astype(F32)).astype(BF16)
        v_ext.append(jnp.concatenate([jnp.concatenate([kv_cur[:, vsl(g)], ones], axis=1),
                                      jnp.concatenate([v_diff, zeros], axis=1),
                                      jnp.concatenate([kv_meta[:, vsl(g)], ones], axis=1)], axis=0))

    s_band, s_meta = [], []
    for ((_, _, _, n), g), s in zip(chains, s_all):
        if prefix:
            s_band.append(jnp.full((ATT_ROWS, BLOCK), neg_inf))
        else:
            s_band.append(jnp.where(upper, s[:, :BLOCK] + jnp.where(n >= 1, 0.0, neg_inf), s[:, BLOCK:2 * BLOCK]))
        s_meta.append(jnp.where(meta_ok, s[:, 2 * BLOCK:], fills[g]))
    m = [jnp.max(jnp.maximum(sb, sm), axis=-1, keepdims=True) for sb, sm in zip(s_band, s_meta)]
    p_all = []
    for sb, sm, mx in zip(s_band, s_meta, m):
        p_band = jnp.exp(sb - mx)
        p_all.append(jnp.concatenate([p_band.astype(BF16), jnp.where(upper, p_band, 0.0).astype(BF16),
                                      jnp.exp(sm - mx).astype(BF16)], axis=1))
    o_ext = [_dot(p, v) for p, v in zip(p_all, v_ext)]
    o = [(x * pltpu.roll(1.0 / x, ATT_HEAD_DIM, 1))[:, :ATT_HEAD_DIM] for x in o_ext]

    outs = []
    for b in range(len(blocks)):
        heads = [o[b * ATT_KV_HEADS + g][i * BLOCK:(i + 1) * BLOCK]
                 for g in range(ATT_KV_HEADS) for i in range(ATT_GROUP)]
        outs.append(jnp.concatenate(heads, axis=1).astype(BF16))
    return outs


def _attention_tile(sink_ref, q_ref, kv_ref, halo_ref, meta_ref, j, tile, prefix):
    nblk = tile // BLOCK
    lane = _iota2((BLOCK, BLOCK), 1)
    fills = [jnp.concatenate([jnp.where(lane == 0, sink_ref[g * ATT_GROUP + i], -jnp.inf)
                              for i in range(ATT_GROUP)], axis=0) for g in range(ATT_KV_HEADS)]
    rows = lambda i: slice(i * BLOCK, (i + 1) * BLOCK)
    blocks = [(q_ref[0, rows(i), :], halo_ref[0] if i == 0 else kv_ref[0, rows(i - 1), :],
               kv_ref[0, rows(i), :], j * nblk + i) for i in range(nblk)]
    return jnp.concatenate(_attn_blocks(blocks, meta_ref[0], fills, prefix), axis=0)


def _inverse_masks():
    C = DN_CHUNK
    ri = _iota2((C, C), 0)
    ci = _iota2((C, C), 1)
    same = lambda size: (ri // size) == (ci // size)
    levels = []
    size = DN_BASE
    while size < C:
        levels.append(same(2 * size) & ~same(size))
        size *= 2
    return (ri == ci).astype(F32), same(DN_BASE), levels


def _unit_lower_inverses(mats, masks):
    C = DN_CHUNK
    eye, base, levels = masks
    ds = [jnp.where(base, a, 0.0) for a in mats]
    ts = [eye - d for d in ds]
    d16 = [d.astype(BF16) for d in ds]
    powers = [_dot(d, d) for d in d16]
    size = 2
    while size < DN_BASE:
        pbs = [p.astype(BF16) for p in powers]
        if 2 * size < DN_BASE:
            both = [_dot(jnp.concatenate([t.astype(BF16), pb], axis=0), pb) for t, pb in zip(ts, pbs)]
            ts = [t + b[:C] for t, b in zip(ts, both)]
            powers = [b[C:] for b in both]
        else:
            ts = [t + _dot(t.astype(BF16), pb) for t, pb in zip(ts, pbs)]
        size *= 2
    for level in levels:
        es = [jnp.where(level, a, 0.0).astype(BF16) for a in mats]
        tbs = [t.astype(BF16) for t in ts]
        tes = [_dot(tb, e).astype(BF16) for tb, e in zip(tbs, es)]
        ts = [t - _dot(te, tb) for t, te, tb in zip(ts, tes, tbs)]
    return ts


def _deltanet_scores(q_ref, k_ref, tile):
    C = DN_CHUNK
    chains = [(c, h) for c in range(tile // C) for h in range(DN_HEADS)]
    q16 = [q_ref[0, c * C:(c + 1) * C, h * DN_HEAD_DIM:(h + 1) * DN_HEAD_DIM] for c, h in chains]
    k16 = [k_ref[0, c * C:(c + 1) * C, h * DN_HEAD_DIM:(h + 1) * DN_HEAD_DIM] for c, h in chains]
    return [_dot_nt(k, k) for k in k16], [_dot_nt(q, k) for q, k in zip(q16, k16)]


def _deltanet_tile(scores, q_ref, k_ref, v_ref, z_ref, pack_ref, nw_ref, state_ref, tile):
    C = DN_CHUNK
    nchunk = tile // C

    ri = _iota2((C, C), 0)
    ci = _iota2((C, C), 1)
    tril = ri >= ci
    strict = ri > ci
    masks = _inverse_masks()

    chains = [(c, h) for c in range(nchunk) for h in range(DN_HEADS)]
    rows = lambda c: slice(c * C, (c + 1) * C)
    lanes = lambda h: slice(h * DN_HEAD_DIM, (h + 1) * DN_HEAD_DIM)
    packs = [pack_ref[0, rows(c), :] for c in range(nchunk)]
    pack_ts = [p.T for p in packs]
    col = lambda c, h, grp: packs[c][:, DN_HEADS * grp + h:DN_HEADS * grp + h + 1]
    row = lambda c, h, grp: pack_ts[c][DN_HEADS * grp + h:DN_HEADS * grp + h + 1, :]

    q16 = [q_ref[0, rows(c), lanes(h)] for c, h in chains]
    k16 = [k_ref[0, rows(c), lanes(h)] for c, h in chains]
    v16 = [v_ref[0, rows(c), lanes(h)] for c, h in chains]
    decay = [jnp.exp(jnp.where(tril, col(c, h, PACK_GCUM) - row(c, h, PACK_GCUM), -jnp.inf)) for c, h in chains]
    kk, qk = scores
    mats = [jnp.where(strict, x * d, 0.0) * col(c, h, PACK_BETA) for x, d, (c, h) in zip(kk, decay, chains)]
    attn = [(x * d).astype(BF16) for x, d in zip(qk, decay)]
    ts = _unit_lower_inverses(mats, masks)
    tb = [(t * row(c, h, PACK_BETA)).astype(BF16) for t, (c, h) in zip(ts, chains)]
    k_eg = [(k.astype(F32) * col(c, h, PACK_EG)).astype(BF16) for k, (c, h) in zip(k16, chains)]
    uw = [_dot(t, jnp.concatenate([v, ke], axis=1)).astype(BF16) for t, v, ke in zip(tb, v16, k_eg)]
    q_dec = [q.astype(F32) * col(c, h, PACK_EG) for q, (c, h) in zip(q16, chains)]
    k_dec = [(k.astype(F32) * col(c, h, PACK_EDEC)).astype(BF16) for k, (c, h) in zip(k16, chains)]
    ktuw = [_dot_tn(kd, x) for kd, x in zip(k_dec, uw)]
    auw = [_dot(a, x) for a, x in zip(attn, uw)]
    lhs = [jnp.concatenate([-kt[:, DN_HEAD_DIM:], qd - a[:, DN_HEAD_DIM:]], axis=0).astype(BF16)
           for kt, qd, a in zip(ktuw, q_dec, auw)]

    ys = []
    for c in range(nchunk):
        idx = [c * DN_HEADS + h for h in range(DN_HEADS)]
        states = [state_ref[h] for h in range(DN_HEADS)]
        both = [_dot(lhs[i], s.astype(BF16)) for i, s in zip(idx, states)]
        for h, (i, s, b) in enumerate(zip(idx, states, both)):
            egl = packs[c][C - 1:C, DN_HEADS * PACK_EGL + h:DN_HEADS * PACK_EGL + h + 1]
            state_ref[h] = s * egl + b[:DN_HEAD_DIM] + ktuw[i][:, :DN_HEAD_DIM]
        outs = [b[DN_HEAD_DIM:] + auw[i][:, :DN_HEAD_DIM] for i, b in zip(idx, both)]
        ys.append(jnp.concatenate([(_rms(o, nw_ref[...]) * _silu(z_ref[0, rows(c), lanes(h)].astype(F32))
                                    ).astype(BF16) for h, o in enumerate(outs)], axis=1))
    return jnp.concatenate(ys, axis=0)


def _tail_kernel(sink_ref, q_ref, kv_ref, halo_ref, meta_ref, dq_ref, dk_ref, dv_ref, dz_ref, pack_ref,
                 dnw_ref, stateinit_ref, h_ref, gate_ref, wdown_ref, nmix_ref, npre_ref, wfi_ref, npost_ref,
                 out_ref, stateout_ref, state_ref, yatt_ref, ydn_ref,
                 *, tile, prefix, n_tiles, tiles_per_seq):
    s = pl.program_id(0)
    j_mix = jnp.minimum(s, n_tiles - 1) % tiles_per_seq

    @pl.when(s == 0)
    def _():
        yatt_ref[...] = jnp.zeros_like(yatt_ref)
        ydn_ref[...] = jnp.zeros_like(ydn_ref)

    @pl.when(j_mix == 0)
    def _():
        state_ref[...] = stateinit_ref[...]

    d_ff = wfi_ref.shape[1] // 2
    r_att, r_dn, r_out = 0, ATT_Q, ATT_Q + DN_W
    r_ffn = r_out + D_MODEL
    wdown = lambda r0, r1: wdown_ref[r0:r1, :D_MODEL]
    ga = gate_ref[0, :, :D_MODEL].astype(F32)
    gb = gate_ref[0, :, D_MODEL:].astype(F32)
    rows = min(tile, DOT_ROWS)
    dot_rows = lambda a, b: jnp.concatenate([_dot(a[r0:r0 + rows], b) for r0 in range(0, tile, rows)], axis=0)
    merged = (_sigmoid(ga) * dot_rows(yatt_ref[...], wdown(r_att, r_dn))
              + _sigmoid(gb) * dot_rows(ydn_ref[...], wdown(r_dn, r_out)))
    mixed = dot_rows(merged.astype(BF16), wdown(r_out, r_ffn))
    dn_scores = _deltanet_scores(dq_ref, dk_ref, tile)
    x = h_ref[0] + _mask_pads(_rms(mixed, nmix_ref[...]), prefix)
    hf = _rms(x, npre_ref[...]).astype(BF16)
    gate_up = dot_rows(hf, wfi_ref[...])
    ffn = dot_rows((_silu(gate_up[:, :d_ff]) * gate_up[:, d_ff:]).astype(BF16), wdown(r_ffn, r_ffn + d_ff))
    out_ref[0] = x + _mask_pads(_rms(ffn, npost_ref[...]), prefix)

    y_dn = _deltanet_tile(dn_scores, dq_ref, dk_ref, dv_ref, dz_ref, pack_ref, dnw_ref, state_ref, tile)
    y_att = _attention_tile(sink_ref, q_ref, kv_ref, halo_ref, meta_ref, j_mix, tile, prefix)
    yatt_ref[...] = y_att
    ydn_ref[...] = y_dn
    stateout_ref[...] = state_ref[...]


def _layer_tail(h, q, kv, kv_meta, dq, dk, dv, dz, pack, gates, sinks, dn_norm_w, state_init, wdown, nmix, npre,
                wfi, npost, layer, *, tile, prefix):
    B, L, D = h.shape
    tiles_per_seq = L // tile
    n_tiles = B * tiles_per_seq
    nblk = tile // BLOCK

    def mix_tile(s):
        t = jnp.minimum(s, n_tiles - 1)
        return t // tiles_per_seq, t % tiles_per_seq

    def ffn_tile(s):
        t = jnp.maximum(s - 1, 0)
        return t // tiles_per_seq, t % tiles_per_seq

    mix_rows = lambda width: pl.BlockSpec((1, tile, width), lambda s: (*mix_tile(s), 0))
    ffn_rows = lambda width: pl.BlockSpec((1, tile, width), lambda s: (*ffn_tile(s), 0))
    halo = pl.BlockSpec((1, BLOCK, 2 * ATT_KV),
                        lambda s: (mix_tile(s)[0], jnp.maximum(mix_tile(s)[1] * nblk - 1, 0), 0))
    state_shape = (DN_HEADS, DN_HEAD_DIM, DN_HEAD_DIM)
    kernel_fn = functools.partial(_tail_kernel, tile=tile, prefix=prefix, n_tiles=n_tiles,
                                  tiles_per_seq=tiles_per_seq)
    return pl.pallas_call(
        kernel_fn,
        grid=(n_tiles + 1,),
        in_specs=[pl.BlockSpec(memory_space=pltpu.SMEM),
                  mix_rows(ATT_Q), mix_rows(2 * ATT_KV), halo, _resident(kv_meta.shape),
                  mix_rows(DN_W), mix_rows(DN_W), mix_rows(DN_W), mix_rows(DN_W), mix_rows(LANES),
                  _resident(dn_norm_w.shape, layer), _resident(state_shape),
                  ffn_rows(D), ffn_rows(2 * D),
                  _resident(wdown.shape, layer), _resident(nmix.shape, layer), _resident(npre.shape, layer),
                  _resident(wfi.shape, layer), _resident(npost.shape, layer)],
        out_specs=[ffn_rows(D), pl.BlockSpec(state_shape, lambda s: (0, 0, 0))],
        out_shape=[jax.ShapeDtypeStruct((B, L, D), F32), jax.ShapeDtypeStruct(state_shape, F32)],
        scratch_shapes=[pltpu.VMEM(state_shape, F32),
                        pltpu.VMEM((tile, ATT_Q), BF16), pltpu.VMEM((tile, DN_W), BF16)],
        compiler_params=pltpu.CompilerParams(
            dimension_semantics=("arbitrary",), vmem_limit_bytes=VMEM_LIMIT),
        name="layer_tail",
    )(sinks, q, kv, kv, kv_meta, dq, dk, dv, dz, pack, dn_norm_w, state_init, h, gates, wdown, nmix, npre, wfi,
      npost)


def _regroup_in_proj(t):
    d0 = ATT_Q + 2 * ATT_KV
    z0 = d0 + 3 * DN_W
    a0 = z0 + DN_W
    g0 = a0 + 2 * DN_HEADS
    reps = (1,) * (t.ndim - 1) + (LANES // (2 * DN_HEADS),)
    return jnp.concatenate([t[..., d0:z0], t[..., :d0], jnp.tile(t[..., a0:g0], reps), t[..., z0:a0],
                            t[..., g0:]], axis=-1)


def kernel(x, meta_tokens, w_in, b_in, conv_w, a_log, dt_bias, dn_norm_w, att_sinks, w_att_out, w_dn_out,
           w_out, norm_mix_pre, norm_mix_post, norm_ffn_pre, norm_ffn_post, w_ffn_in, w_ffn_out):
    depth = w_in.shape[0]
    h_prefix = jnp.concatenate([jnp.zeros((PAD_FRONT, D_MODEL), x.dtype), meta_tokens.astype(x.dtype)])[None]
    h_main = x

    rows = lambda v: v.astype(F32)[:, None, :]
    lane_tiled = lambda v: jnp.tile(v.astype(F32), (1, LANES // DN_HEADS))[:, None, :]
    w_main = _regroup_in_proj(w_in).astype(BF16)
    b_main = rows(_regroup_in_proj(b_in))
    wdown = jnp.pad(jnp.concatenate([w_att_out, w_dn_out, w_out, w_ffn_out], axis=1).astype(BF16),
                    ((0, 0), (0, 0), (0, LANES)))
    wfi = w_ffn_in.astype(BF16)
    conv_w32, alog, dtb = conv_w.astype(F32), lane_tiled(a_log), lane_tiled(dt_bias)
    sinks = att_sinks.astype(F32)
    n_pre, n_dn, n_mix, n_ffn_pre, n_ffn_post = (rows(v) for v in (
        norm_mix_pre, dn_norm_w, norm_mix_post, norm_ffn_pre, norm_ffn_post))
    no_history = jnp.zeros((1, SUBLANES, 3 * DN_W), F32)
    no_state = jnp.zeros((DN_HEADS, DN_HEAD_DIM, DN_HEAD_DIM), F32)

    def layer(h, l, kv_meta, conv_init, state_init, tile, prefix):
        q, kv, dq, dk, dv, dz, gates, pack, conv_tail = _inproj(
            h, n_pre, w_main, b_main, conv_w32, alog, dtb, conv_init, l, tile=tile, prefix=prefix)
        h, state = _layer_tail(h, q, kv, kv if prefix else kv_meta, dq, dk, dv, dz, pack, gates, sinks[l], n_dn,
                               state_init, wdown, n_mix, n_ffn_pre, wfi, n_ffn_post, l, tile=tile, prefix=prefix)
        return h, kv, conv_tail, state

    for l in range(depth):
        h_prefix, kv_prefix, conv_prefix, state_prefix = layer(h_prefix, l, None, no_history, no_state,
                                                               PREFIX, True)
        h_main, _, _, _ = layer(h_main, l, kv_prefix, conv_prefix, state_prefix, TILE, False)
    return h_main
```
